```python
import jax, jax.numpy as jnp
from jax import lax
import numpy as np

D_MODEL = 4096
BATCH = 2
SEQ = 4096
DEPTH = 1
DEC_BATCH = 8
DEC_SEQ = 32
PAST_LEN = 4096

CHUNK = 64
SB_Q_BLOCK = 128
SB_HEAD_DIM = 128
SB_WIDTH = D_MODEL // 2
SB_HEADS = SB_WIDTH // SB_HEAD_DIM
ML_V_DIM = 512
ML_QK_DIM = ML_V_DIM // 2
ML_V_WIDTH = D_MODEL - SB_WIDTH
ML_HEADS = ML_V_WIDTH // ML_V_DIM
ML_QK_WIDTH = ML_HEADS * ML_QK_DIM
MIX_WIDTH = SB_WIDTH + ML_V_WIDTH
D_FF = 256 * ((8 * D_MODEL // 3 + 255) // 256)
PLE_DIM = 256
EPS = 1e-6
F_BIAS_LO = 3.0
F_BIAS_HI = 6.0
IN_WIDTHS = (SB_WIDTH, SB_WIDTH, SB_WIDTH, ML_QK_WIDTH, ML_QK_WIDTH, ML_V_WIDTH, ML_V_WIDTH, 2 * ML_HEADS)
IN_WIDTH = sum(IN_WIDTHS)
IN_SPLITS = tuple(int(s) for s in np.cumsum(IN_WIDTHS)[:-1])

kernel_name = 'hybrid_stickbreak_mlstm_streaming_step'

F32 = jnp.float32


def rmsnorm(x, g):
    xf = x.astype(F32)
    y = xf * lax.rsqrt(jnp.mean(xf * xf, axis=-1, keepdims=True) + EPS)
    return (y * g.astype(F32)).astype(x.dtype)


def half_ffn(x, ln, w_gate, w_up, w_down):
    h = rmsnorm(x, ln)
    return x + 0.5 * ((jax.nn.silu(h @ w_gate) * (h @ w_up)) @ w_down)


def mixer_heads(h, w_in, b_if):
    B, L, _ = h.shape
    z = h @ w_in
    sb_q, sb_k, sb_v, ml_q, ml_k, ml_v, ml_o, ml_g = jnp.split(z, IN_SPLITS, axis=-1)
    sb_q, sb_k, sb_v = [a.reshape(B, L, SB_HEADS, SB_HEAD_DIM) for a in (sb_q, sb_k, sb_v)]

    def ml_heads(a, d):
        return jnp.transpose(a.reshape(B, L, ML_HEADS, d), (0, 2, 1, 3)).astype(F32)

    ml_q = ml_heads(ml_q, ML_QK_DIM) * (ML_QK_DIM ** -0.5)
    ml_k = ml_heads(ml_k, ML_QK_DIM)
    ml_v = ml_heads(ml_v, ML_V_DIM)
    gates = jnp.transpose((ml_g + b_if).astype(F32), (0, 2, 1))
    i_pre = gates[:, :ML_HEADS]
    log_f = jax.nn.log_sigmoid(gates[:, ML_HEADS:])
    return sb_q, sb_k, sb_v, ml_q, ml_k, ml_v, ml_o, i_pre, log_f


def sb_block(q, k, v, q_pos, k_pos):
    z = jnp.einsum('bqhd,bkhd->bhqk', q, k).astype(F32) * (SB_HEAD_DIM ** -0.5)
    mask = k_pos[None, :] < q_pos[:, None]
    log_1m = jnp.where(mask, -jax.nn.softplus(z), 0.0)
    after = lax.cumsum(log_1m, axis=3, reverse=True) - log_1m
    a = jnp.where(mask, jnp.exp(jax.nn.log_sigmoid(z) + after), 0.0)
    return jnp.einsum('bhqk,bkhd->bqhd', a, v.astype(F32))


def sb_prompt(q, k, v):
    B, S = q.shape[0], q.shape[1]
    nb = S // SB_Q_BLOCK
    pos = jnp.arange(S)
    qb = jnp.moveaxis(q.reshape(B, nb, SB_Q_BLOCK, SB_HEADS, SB_HEAD_DIM), 1, 0)
    pb = pos.reshape(nb, SB_Q_BLOCK)
    out = lax.map(lambda a: sb_block(a[0], k, v, a[1], pos), (qb, pb))
    return jnp.moveaxis(out, 0, 1).reshape(B, S, SB_HEADS, SB_HEAD_DIM)


def sb_step(q, k_new, v_new, k_cache, v_cache):
    P, L = k_cache.shape[1], q.shape[1]
    k = jnp.concatenate([k_cache, k_new.astype(k_cache.dtype)], axis=1)
    v = jnp.concatenate([v_cache, v_new.astype(v_cache.dtype)], axis=1)
    return sb_block(q, k, v, P + jnp.arange(L), jnp.arange(P + L))


def mlstm_chunk(carry, q, k, v, i_pre, log_f):
    C, n, m = carry
    L = q.shape[2]
    b = jnp.cumsum(log_f, axis=-1)
    causal = jnp.tril(jnp.ones((L, L), dtype=bool))
    d_log = jnp.where(causal, b[..., :, None] - b[..., None, :] + i_pre[..., None, :], -jnp.inf)
    inter_log = b + m[..., None]
    m_row = jnp.maximum(inter_log, jnp.max(d_log, axis=-1))
    w_intra = jnp.exp(d_log - m_row[..., None])
    w_inter = jnp.exp(inter_log - m_row)
    s = jnp.einsum('bhtd,bhsd->bhts', q, k) * w_intra
    num = w_inter[..., None] * jnp.einsum('bhvd,bhtd->bhtv', C, q) + jnp.einsum('bhts,bhsv->bhtv', s, v)
    den = w_inter * jnp.einsum('bhd,bhtd->bht', n, q) + jnp.sum(s, axis=-1)
    h = num / jnp.maximum(jnp.abs(den), jnp.exp(-m_row))[..., None]
    m_new = m_row[..., -1]
    w_prev = jnp.exp(b[..., -1] + m - m_new)
    w_tok = jnp.exp(b[..., -1:] - b + i_pre - m_new[..., None])
    C_new = w_prev[..., None, None] * C + jnp.einsum('bhs,bhsv,bhsd->bhvd', w_tok, v, k)
    n_new = w_prev[..., None] * n + jnp.einsum('bhs,bhsd->bhd', w_tok, k)
    return (C_new, n_new, m_new), h


def mlstm_prompt(q, k, v, i_pre, log_f):
    B, H, S = q.shape[0], q.shape[1], q.shape[2]
    nc = S // CHUNK

    def to_chunks(a):
        return jnp.moveaxis(a.reshape(a.shape[:2] + (nc, CHUNK) + a.shape[3:]), 2, 0)

    carry0 = (jnp.zeros((B, H, ML_V_DIM, ML_QK_DIM), F32),
              jnp.zeros((B, H, ML_QK_DIM), F32),
              jnp.zeros((B, H), F32))
    carry, h = lax.scan(lambda c, xs: mlstm_chunk(c, *xs), carry0,
                        (to_chunks(q), to_chunks(k), to_chunks(v), to_chunks(i_pre), to_chunks(log_f)))
    h = jnp.moveaxis(h, 0, 2).reshape(B, H, S, ML_V_DIM)
    return carry, h


def layer_pre(x, ln1, w1g, w1u, w1d, ln_mix, w_in, b_if):
    x = half_ffn(x, ln1, w1g, w1u, w1d)
    return x, mixer_heads(rmsnorm(x, ln_mix), w_in, b_if)


def layer_post(x, sb_h, ml_h, ml_o, p, g_sb, g_ml, w_out, ln2, w2g, w2u, w2d, ln_ple, w_pg, w_pp):
    B, L, _ = x.shape
    sb = rmsnorm(sb_h, g_sb).reshape(B, L, SB_WIDTH)
    ml = rmsnorm(jnp.transpose(ml_h, (0, 2, 1, 3)), g_ml) * jax.nn.sigmoid(
        ml_o.reshape(B, L, ML_HEADS, ML_V_DIM).astype(F32))
    mixed = jnp.concatenate([sb, ml.reshape(B, L, ML_V_WIDTH)], axis=-1).astype(x.dtype)
    x = x + mixed @ w_out
    x = half_ffn(x, ln2, w2g, w2u, w2d)
    gate = jax.nn.sigmoid(rmsnorm(x, ln_ple) @ w_pg)
    return x + gate * (p @ w_pp)


def setup_inputs(seed: int = 0) -> dict:
    key = jax.random.key(seed)
    ks = iter(jax.random.split(key, 40))

    def nrm(shape, scale=1.0):
        return jax.random.normal(next(ks), shape, F32) * scale

    def gain(shape):
        return 1.0 + nrm(shape, 0.02)

    D, F = D_MODEL, D_FF
    b_i = nrm((DEPTH, ML_HEADS), 0.1)
    b_f = jnp.linspace(F_BIAS_LO, F_BIAS_HI, ML_HEADS, dtype=F32)[None, :] + nrm((DEPTH, ML_HEADS), 0.1)
    b_if = jnp.concatenate([b_i, b_f], axis=-1)
    return {
        'x_prompt': nrm((BATCH, SEQ, D)),
        'x_sample': nrm((DEC_BATCH, DEC_SEQ, D)),
        'cache_sb_k': nrm((DEPTH, DEC_BATCH, PAST_LEN, SB_HEADS, SB_HEAD_DIM)),
        'cache_sb_v': nrm((DEPTH, DEC_BATCH, PAST_LEN, SB_HEADS, SB_HEAD_DIM)),
        'state_ml_c': nrm((DEPTH, DEC_BATCH, ML_HEADS, ML_V_DIM, ML_QK_DIM)),
        'state_ml_n': nrm((DEPTH, DEC_BATCH, ML_HEADS, ML_QK_DIM)),
        'state_ml_m': nrm((DEPTH, DEC_BATCH, ML_HEADS)),
        'p_prompt': nrm((DEPTH, BATCH, SEQ, PLE_DIM)),
        'p_sample': nrm((DEPTH, DEC_BATCH, DEC_SEQ, PLE_DIM)),
        'ln_ffn1': gain((DEPTH, D)),
        'w_ffn1_gate': nrm((DEPTH, D, F), D ** -0.5),
        'w_ffn1_up': nrm((DEPTH, D, F), D ** -0.5),
        'w_ffn1_down': nrm((DEPTH, F, D), F ** -0.5),
        'ln_mix': gain((DEPTH, D)),
        'w_in': nrm((DEPTH, D, IN_WIDTH), D ** -0.5),
        'b_if': b_if,
        'g_sb_head': gain((DEPTH, SB_HEADS, SB_HEAD_DIM)),
        'g_ml_head': gain((DEPTH, ML_HEADS, ML_V_DIM)),
        'w_out': nrm((DEPTH, MIX_WIDTH, D), MIX_WIDTH ** -0.5),
        'ln_ffn2': gain((DEPTH, D)),
        'w_ffn2_gate': nrm((DEPTH, D, F), D ** -0.5),
        'w_ffn2_up': nrm((DEPTH, D, F), D ** -0.5),
        'w_ffn2_down': nrm((DEPTH, F, D), F ** -0.5),
        'ln_ple': gain((DEPTH, D)),
        'w_ple_gate': nrm((DEPTH, D, D), D ** -0.5),
        'w_ple_proj': nrm((DEPTH, PLE_DIM, D), PLE_DIM ** -0.5),
        'ln_final': gain((D,)),
    }


def reference(x_prompt, x_sample, cache_sb_k, cache_sb_v, state_ml_c, state_ml_n, state_ml_m,
              p_prompt, p_sample, ln_ffn1, w_ffn1_gate, w_ffn1_up, w_ffn1_down, ln_mix, w_in, b_if,
              g_sb_head, g_ml_head, w_out, ln_ffn2, w_ffn2_gate, w_ffn2_up, w_ffn2_down,
              ln_ple, w_ple_gate, w_ple_proj, ln_final):
    xp, xs = x_prompt, x_sample
    pk, pv, pc, pn, pm = [], [], [], [], []
    sk_l, sv_l, sc, sn, sm = [], [], [], [], []
    for l in range(DEPTH):
        pre_w = (ln_ffn1[l], w_ffn1_gate[l], w_ffn1_up[l], w_ffn1_down[l], ln_mix[l], w_in[l], b_if[l])
        post_w = (g_sb_head[l], g_ml_head[l], w_out[l], ln_ffn2[l], w_ffn2_gate[l], w_ffn2_up[l],
                  w_ffn2_down[l], ln_ple[l], w_ple_gate[l], w_ple_proj[l])
        xp, (sq, sk, sv, mq, mk, mv, mo, mi, mf) = layer_pre(xp, *pre_w)
        sb_h = sb_prompt(sq, sk, sv)
        (c, n, m), ml_h = mlstm_prompt(mq, mk, mv, mi, mf)
        xp = layer_post(xp, sb_h, ml_h, mo, p_prompt[l], *post_w)
        pk.append(sk); pv.append(sv); pc.append(c); pn.append(n); pm.append(m)
        xs, (sq, sk, sv, mq, mk, mv, mo, mi, mf) = layer_pre(xs, *pre_w)
        sb_h = sb_step(sq, sk, sv, cache_sb_k[l], cache_sb_v[l])
        carry = (state_ml_c[l].astype(F32), state_ml_n[l].astype(F32), state_ml_m[l].astype(F32))
        (c, n, m), ml_h = mlstm_chunk(carry, mq, mk, mv, mi, mf)
        xs = layer_post(xs, sb_h, ml_h, mo, p_sample[l], *post_w)
        sk_l.append(sk); sv_l.append(sv); sc.append(c); sn.append(n); sm.append(m)
    y_prompt = rmsnorm(xp, ln_final)
    y_sample = rmsnorm(xs, ln_final)
    return (y_prompt, y_sample,
            jnp.stack(pk), jnp.stack(pv), jnp.stack(pc), jnp.stack(pn), jnp.stack(pm),
            jnp.stack(sk_l), jnp.stack(sv_l), jnp.stack(sc), jnp.stack(sn), jnp.stack(sm))
```

```python
import functools

import jax
import jax.numpy as jnp
from jax import lax
from jax.experimental import pallas as pl
from jax.experimental.pallas import tpu as pltpu

F32 = jnp.float32
BF16 = jnp.bfloat16
EPS = 1e-6

V7X_VMEM_BYTES = 64 * 1024 * 1024
V7X_LANES = 128
V7X_MXU_DIM = 256
VMEM_REQUEST_CAP = V7X_VMEM_BYTES - 6 * 1024 * 1024

SB_HEAD_DIM = 128
ML_QK_DIM = 256
ML_V_DIM = 512
ML_HEADS = 4
N_GATES = 2 * ML_HEADS


def _vmem_limit(block_bytes, scratch_bytes=0, temp_bytes=0):
    est = 2 * block_bytes + scratch_bytes + temp_bytes + (2 << 20)
    return int(min(max(est, 16 << 20), VMEM_REQUEST_CAP))


def _nbytes(shape, dtype):
    n = 1
    for s in shape:
        n *= s
    return n * jnp.dtype(dtype).itemsize


def _sigmoid(x):
    return 1.0 / (1.0 + jnp.exp(-x))


def _softplus(x):
    return jnp.maximum(x, 0.0) + jnp.log1p(jnp.exp(-jnp.abs(x)))


def _rmsnorm_kernel(x_ref, g_ref, o_ref):
    x = x_ref[...]
    y = x * lax.rsqrt(jnp.mean(x * x, axis=-1, keepdims=True) + EPS)
    o_ref[...] = (y * g_ref[...]).astype(o_ref.dtype)


def _rmsnorm(x, g, out_dtype, *, rows, row_tile, row_offset=0):
    d = x.shape[1]
    off = row_offset // row_tile
    blocks = _nbytes((row_tile, d), F32) + _nbytes((row_tile, d), out_dtype)
    return pl.pallas_call(
        _rmsnorm_kernel,
        out_shape=jax.ShapeDtypeStruct((rows, d), out_dtype),
        grid=(rows // row_tile,),
        in_specs=[pl.BlockSpec((row_tile, d), lambda i: (i + off, 0)),
                  pl.BlockSpec((1, d), lambda i: (0, 0))],
        out_specs=pl.BlockSpec((row_tile, d), lambda i: (i, 0)),
        compiler_params=pltpu.CompilerParams(
            dimension_semantics=("parallel",),
            vmem_limit_bytes=_vmem_limit(blocks, temp_bytes=2 * _nbytes((row_tile, d), F32))),
        name="rmsnorm",
    )(x, g.reshape(1, d))


def _mm_kernel(*refs, n_w, n_extra, epilogue):
    lhs_ref = refs[0]
    w_refs = refs[1:1 + n_w]
    extra_refs = refs[1 + n_w:1 + n_w + n_extra]
    out_refs = refs[1 + n_w + n_extra:]
    lhs = lhs_ref[...]
    accs = [jnp.dot(lhs, w[...].astype(BF16), preferred_element_type=F32) for w in w_refs]
    outs = epilogue(accs, [e[...] for e in extra_refs])
    for o_ref, o in zip(out_refs, outs):
        o_ref[...] = o.astype(o_ref.dtype)


def _matmul_fullk(lhs, ws, col_offsets, n_cols, out_dtypes, epilogue, extras=(), *,
                  tm, tn, rows=None, row_offset=0, name):
    k_dim = lhs.shape[1]
    rows = lhs.shape[0] if rows is None else rows
    roff = row_offset // tm
    in_specs = [pl.BlockSpec((tm, k_dim), lambda i, j: (i + roff, 0))]
    for off in col_offsets:
        in_specs.append(pl.BlockSpec((k_dim, tn), lambda i, j, o=off // tn: (0, j + o)))
    for _ in extras:
        in_specs.append(pl.BlockSpec((tm, tn), lambda i, j: (i + roff, j)))
    out_specs = [pl.BlockSpec((tm, tn), lambda i, j: (i, j)) for _ in out_dtypes]
    out_shape = [jax.ShapeDtypeStruct((rows, n_cols), dt) for dt in out_dtypes]
    blocks = (_nbytes((tm, k_dim), BF16) + len(ws) * _nbytes((k_dim, tn), F32)
              + len(extras) * _nbytes((tm, tn), F32) + sum(_nbytes((tm, tn), dt) for dt in out_dtypes))
    temps = len(ws) * (_nbytes((k_dim, tn), BF16) + 2 * _nbytes((tm, tn), F32))
    return pl.pallas_call(
        functools.partial(_mm_kernel, n_w=len(ws), n_extra=len(extras), epilogue=epilogue),
        out_shape=out_shape,
        grid=(rows // tm, n_cols // tn),
        in_specs=in_specs,
        out_specs=out_specs,
        compiler_params=pltpu.CompilerParams(
            dimension_semantics=("parallel", "arbitrary"),
            vmem_limit_bytes=_vmem_limit(blocks, temp_bytes=temps)),
        name=name,
    )(lhs, *ws, *extras)


def _ep_swiglu(accs, extras):
    g, u = accs
    return [g * _sigmoid(g) * u]


def _ep_identity(accs, extras):
    return list(accs)


def _ep_residual(accs, extras):
    return [extras[0] + accs[0]]


def _ep_ml_qk(accs, extras):
    q, k = accs
    return [q * (ML_QK_DIM ** -0.5), k]


def _down_kernel(a_ref, w_ref, x_ref, o_ref, acc_ref, *, nk, k_valid_last, tk):
    k = pl.program_id(2)

    @pl.when(k == 0)
    def _():
        acc_ref[...] = jnp.zeros_like(acc_ref)

    def step(masked):
        a = a_ref[...]
        w = w_ref[...]
        if masked:
            a = jnp.where(lax.broadcasted_iota(jnp.int32, a.shape, 1) < k_valid_last, a, jnp.zeros_like(a))
            w = jnp.where(lax.broadcasted_iota(jnp.int32, w.shape, 0) < k_valid_last, w, jnp.zeros_like(w))
        acc_ref[...] += jnp.dot(a, w.astype(BF16), preferred_element_type=F32)

    if k_valid_last == tk:
        step(False)
    else:
        pl.when(k < nk - 1)(lambda: step(False))
        pl.when(k == nk - 1)(lambda: step(True))

    @pl.when(k == nk - 1)
    def _():
        o_ref[...] = x_ref[...] + 0.5 * acc_ref[...]


def _ffn_down(a, w_down, x, *, tm, tn, tk):
    t, f = a.shape
    d = w_down.shape[1]
    nk = pl.cdiv(f, tk)
    k_valid_last = f - (nk - 1) * tk
    blocks = (_nbytes((tm, tk), BF16) + _nbytes((tk, tn), F32) + 2 * _nbytes((tm, tn), F32))
    return pl.pallas_call(
        functools.partial(_down_kernel, nk=nk, k_valid_last=k_valid_last, tk=tk),
        out_shape=jax.ShapeDtypeStruct((t, d), F32),
        grid=(t // tm, d // tn, nk),
        in_specs=[pl.BlockSpec((tm, tk), lambda i, j, k: (i, k)),
                  pl.BlockSpec((tk, tn), lambda i, j, k: (k, j)),
                  pl.BlockSpec((tm, tn), lambda i, j, k: (i, j))],
        out_specs=pl.BlockSpec((tm, tn), lambda i, j, k: (i, j)),
        scratch_shapes=[pltpu.VMEM((tm, tn), F32)],
        compiler_params=pltpu.CompilerParams(
            dimension_semantics=("parallel", "parallel", "arbitrary"),
            vmem_limit_bytes=_vmem_limit(blocks, scratch_bytes=_nbytes((tm, tn), F32),
                                         temp_bytes=_nbytes((tk, tn), BF16) + _nbytes((tm, tn), F32))),
        name="ffn_down",
    )(a, w_down, x)


def _ple_kernel(h_ref, wg_ref, p_ref, wp_ref, x_ref, o_ref):
    gate = _sigmoid(jnp.dot(h_ref[...], wg_ref[...].astype(BF16), preferred_element_type=F32))
    proj = jnp.dot(p_ref[...].astype(BF16), wp_ref[...].astype(BF16), preferred_element_type=F32)
    o_ref[...] = x_ref[...] + gate * proj


def _ple(h, w_gate, p, w_proj, x, *, tm, tn):
    t, d = h.shape
    pd = p.shape[1]
    blocks = (_nbytes((tm, d), BF16) + _nbytes((d, tn), F32) + _nbytes((tm, pd), F32)
              + _nbytes((pd, tn), F32) + 2 * _nbytes((tm, tn), F32))
    return pl.pallas_call(
        _ple_kernel,
        out_shape=jax.ShapeDtypeStruct((t, d), F32),
        grid=(t // tm, d // tn),
        in_specs=[pl.BlockSpec((tm, d), lambda i, j: (i, 0)),
                  pl.BlockSpec((d, tn), lambda i, j: (0, j)),
                  pl.BlockSpec((tm, pd), lambda i, j: (i, 0)),
                  pl.BlockSpec((pd, tn), lambda i, j: (0, j)),
                  pl.BlockSpec((tm, tn), lambda i, j: (i, j))],
        out_specs=pl.BlockSpec((tm, tn), lambda i, j: (i, j)),
        compiler_params=pltpu.CompilerParams(
            dimension_semantics=("parallel", "arbitrary"),
            vmem_limit_bytes=_vmem_limit(blocks, temp_bytes=_nbytes((d, tn), BF16) + 3 * _nbytes((tm, tn), F32))),
        name="ple",
    )(h, w_gate, p, w_proj, x)


def _gates_kernel(h_ref, w_ref, wt_ref, b_ref, bt_ref, g_ref, gt_ref):
    h = h_ref[...]
    g_ref[...] = jnp.dot(h, w_ref[...].astype(BF16), preferred_element_type=F32) + b_ref[...]
    gt = lax.dot_general(wt_ref[...].astype(BF16), h, (((1,), (1,)), ((), ())), preferred_element_type=F32)
    gt_ref[...] = gt + bt_ref[:, 0:1]


def _gates(h, w_pad, w_t, b_row, b_col, *, rows, chunk, row_offset):
    d = h.shape[1]
    off = row_offset // chunk
    n_chunks = rows // chunk
    return pl.pallas_call(
        _gates_kernel,
        out_shape=[jax.ShapeDtypeStruct((rows, V7X_LANES), F32),
                   jax.ShapeDtypeStruct((n_chunks, N_GATES, chunk), F32)],
        grid=(n_chunks,),
        in_specs=[pl.BlockSpec((chunk, d), lambda i: (i + off, 0)),
                  pl.BlockSpec((d, V7X_LANES), lambda i: (0, 0)),
                  pl.BlockSpec((N_GATES, d), lambda i: (0, 0)),
                  pl.BlockSpec((1, V7X_LANES), lambda i: (0, 0)),
                  pl.BlockSpec((N_GATES, V7X_LANES), lambda i: (0, 0))],
        out_specs=[pl.BlockSpec((chunk, V7X_LANES), lambda i: (i, 0)),
                   pl.BlockSpec((None, N_GATES, chunk), lambda i: (i, 0, 0))],
        compiler_params=pltpu.CompilerParams(dimension_semantics=("parallel",)),
        name="ml_gates",
    )(h, w_pad, w_t, b_row, b_col)


def _split2(x):
    hi = x.astype(BF16)
    lo = (x - hi.astype(F32)).astype(BF16)
    return hi, lo


def _split3(x):
    hi = x.astype(BF16)
    r = x - hi.astype(F32)
    mid = r.astype(BF16)
    lo = (r - mid.astype(F32)).astype(BF16)
    return hi, mid, lo


def _later_key_matrix(tk):
    r = lax.broadcasted_iota(jnp.int32, (tk, tk), 0)
    c = lax.broadcasted_iota(jnp.int32, (tk, tk), 1)
    return (r > c).astype(BF16)


def _sb_block(q, kblk, vblk, later, run, acc, *, scale, diagonal):
    z = lax.dot_general(q, kblk, (((1,), (1,)), ((), ())), preferred_element_type=F32) * scale
    sp = _softplus(z)
    log_beta = z - sp
    log_1m = -sp
    if diagonal:
        row = lax.broadcasted_iota(jnp.int32, z.shape, 0)
        col = lax.broadcasted_iota(jnp.int32, z.shape, 1)
        mask = col < row
        log_1m = jnp.where(mask, log_1m, 0.0)
    hi, lo = _split2(log_1m)
    after = (jnp.dot(hi, later, preferred_element_type=F32)
             + jnp.dot(lo, later, preferred_element_type=F32)) + run
    a = jnp.exp(log_beta + after)
    if diagonal:
        a = jnp.where(mask, a, 0.0)
    acc = acc + jnp.dot(a.astype(BF16), vblk, preferred_element_type=F32)
    run = after[:, 0:1] + log_1m[:, 0:1]
    return run, acc


def _head_rmsnorm(x, g):
    return x * lax.rsqrt(jnp.mean(x * x, axis=-1, keepdims=True) + EPS) * g


def _sb_prompt_kernel(q_ref, k_ref, v_ref, g_ref, mixed_ref, o_ref, *, tile, scale):
    del mixed_ref
    qi = pl.program_id(2)
    q = q_ref[...]
    later = _later_key_matrix(tile)
    d = q.shape[1]

    def visit(kb, carry, diagonal):
        start = pl.multiple_of(kb * tile, tile)
        kblk = k_ref[pl.ds(start, tile), :].astype(BF16)
        vblk = v_ref[pl.ds(start, tile), :].astype(BF16)
        return _sb_block(q, kblk, vblk, later, *carry, scale=scale, diagonal=diagonal)

    carry = (jnp.zeros((tile, 1), F32), jnp.zeros((tile, d), F32))
    carry = visit(qi, carry, True)
    carry = lax.fori_loop(0, qi, lambda t, c: visit(qi - 1 - t, c, False), carry)
    o_ref[...] = _head_rmsnorm(carry[1], g_ref[...]).astype(o_ref.dtype)


def _sb_prompt(q, k, v, g_heads, mixed, *, batch, seq, tile):
    d = SB_HEAD_DIM
    heads = k.shape[1] // d
    nq = seq // tile
    blocks = (2 * _nbytes((tile, d), BF16) + 2 * _nbytes((seq, d), F32) + _nbytes((1, d), F32))
    return pl.pallas_call(
        functools.partial(_sb_prompt_kernel, tile=tile, scale=d ** -0.5),
        out_shape=jax.ShapeDtypeStruct(mixed.shape, mixed.dtype),
        grid=(batch, heads, nq),
        in_specs=[pl.BlockSpec((tile, d), lambda b, h, i: (b * nq + i, h)),
                  pl.BlockSpec((seq, d), lambda b, h, i: (b, h)),
                  pl.BlockSpec((seq, d), lambda b, h, i: (b, h)),
                  pl.BlockSpec((None, 1, d), lambda b, h, i: (h, 0, 0)),
                  pl.BlockSpec(memory_space=pl.ANY)],
        out_specs=pl.BlockSpec((tile, d), lambda b, h, i: (b * nq + i, h)),
        input_output_aliases={4: 0},
        compiler_params=pltpu.CompilerParams(
            dimension_semantics=("parallel", "parallel", "arbitrary"),
            vmem_limit_bytes=_vmem_limit(blocks, temp_bytes=16 * _nbytes((tile, tile), F32))),
        name="sb_prompt",
    )(q, k, v, g_heads.reshape(heads, 1, d), mixed)


def _sb_sample_kernel(q_ref, kn_ref, vn_ref, kc_ref, vc_ref, g_ref, mixed_ref, o_ref, run_ref, acc_ref,
                      *, heads, tile, scale):
    del mixed_ref
    kc = pl.program_id(1)
    nkc = pl.num_programs(1)
    d = SB_HEAD_DIM
    n_new = kn_ref.shape[0]

    @pl.when(kc == 0)
    def _():
        later_new = _later_key_matrix(n_new)
        for h in range(heads):
            cols = slice(h * d, (h + 1) * d)
            run, acc = _sb_block(q_ref[:, cols], kn_ref[:, cols].astype(BF16), vn_ref[:, cols].astype(BF16),
                                 later_new, jnp.zeros((n_new, 1), F32), jnp.zeros((n_new, d), F32),
                                 scale=scale, diagonal=True)
            run_ref[:, cols] = jnp.broadcast_to(run, (n_new, d))
            acc_ref[:, cols] = acc

    later = _later_key_matrix(tile)
    for h in range(heads):
        cols = slice(h * d, (h + 1) * d)
        run, acc = _sb_block(q_ref[:, cols], kc_ref[:, cols].astype(BF16), vc_ref[:, cols].astype(BF16),
                             later, run_ref[:, h * d:h * d + 1], acc_ref[:, cols],
                             scale=scale, diagonal=False)
        run_ref[:, cols] = jnp.broadcast_to(run, (n_new, d))
        acc_ref[:, cols] = acc

    @pl.when(kc == nkc - 1)
    def _():
        for h in range(heads):
            cols = slice(h * d, (h + 1) * d)
            o_ref[:, cols] = _head_rmsnorm(acc_ref[:, cols], g_ref[h:h + 1, :]).astype(o_ref.dtype)


def _sb_sample(q, k_new, v_new, k_cache, v_cache, g_heads, mixed, *, row_offset, tile):
    d = SB_HEAD_DIM
    batch, past, width = k_cache.shape
    heads = width // d
    n_new = k_new.shape[0] // batch
    roff = row_offset // n_new
    nkc = past // tile
    blocks = (_nbytes((n_new, width), BF16) * 2 + 2 * _nbytes((n_new, width), F32)
              + 2 * _nbytes((tile, width), F32) + _nbytes((heads, d), F32))
    return pl.pallas_call(
        functools.partial(_sb_sample_kernel, heads=heads, tile=tile, scale=d ** -0.5),
        out_shape=jax.ShapeDtypeStruct(mixed.shape, mixed.dtype),
        grid=(batch, nkc),
        in_specs=[pl.BlockSpec((n_new, width), lambda b, c: (b + roff, 0)),
                  pl.BlockSpec((n_new, width), lambda b, c: (b, 0)),
                  pl.BlockSpec((n_new, width), lambda b, c: (b, 0)),
                  pl.BlockSpec((None, tile, width), lambda b, c: (b, nkc - 1 - c, 0)),
                  pl.BlockSpec((None, tile, width), lambda b, c: (b, nkc - 1 - c, 0)),
                  pl.BlockSpec((heads, d), lambda b, c: (0, 0)),
                  pl.BlockSpec(memory_space=pl.ANY)],
        out_specs=pl.BlockSpec((n_new, width), lambda b, c: (b + roff, 0)),
        scratch_shapes=[pltpu.VMEM((n_new, width), F32), pltpu.VMEM((n_new, width), F32)],
        input_output_aliases={6: 0},
        compiler_params=pltpu.CompilerParams(
            dimension_semantics=("parallel", "arbitrary"),
            vmem_limit_bytes=_vmem_limit(blocks, scratch_bytes=2 * _nbytes((n_new, width), F32),
                                         temp_bytes=8 << 20)),
        name="sb_sample",
    )(q, k_new, v_new, k_cache, v_cache, g_heads, mixed)


def _log_sigmoid(x):
    return jnp.minimum(x, 0.0) - jnp.log1p(jnp.exp(-jnp.abs(x)))


def _mlstm_kernel(q_ref, k_ref, v_ref, o_ref, g_ref, gt_ref, gain_ref, c0_ref, n0_ref, m0_ref, mixed_ref,
                  out_ref, c_out_ref, n_out_ref, m_out_ref, c_sc, n_sc, m_sc, *, chunk):
    del mixed_ref
    ci = pl.program_id(1)
    nci = pl.num_programs(1)

    @pl.when(ci == 0)
    def _():
        c_sc[...] = c0_ref[...]
        n_sc[...] = n0_ref[...]
        m_sc[...] = m0_ref[...]

    row = lax.broadcasted_iota(jnp.int32, (chunk, chunk), 0)
    col = lax.broadcasted_iota(jnp.int32, (chunk, chunk), 1)
    causal = col <= row
    tri = causal.astype(BF16)
    tri_t = (row <= col).astype(BF16)

    g = g_ref[...]
    gt = gt_ref[...]
    b_cols = sum(jnp.dot(tri, part, preferred_element_type=F32) for part in _split3(_log_sigmoid(g)))
    b_rows = sum(jnp.dot(part, tri_t, preferred_element_type=F32) for part in _split3(_log_sigmoid(gt)))

    for h in range(ML_HEADS):
        qk = slice(h * ML_QK_DIM, (h + 1) * ML_QK_DIM)
        vv = slice(h * ML_V_DIM, (h + 1) * ML_V_DIM)
        q = q_ref[:, qk]
        k = k_ref[:, qk]
        v = v_ref[:, vv]
        c_prev = c_sc[h]
        n_prev = n_sc[h:h + 1, :]
        m_prev = m_sc[h:h + 1, 0:1]

        b_col = b_cols[:, ML_HEADS + h:ML_HEADS + h + 1]
        i_col = g[:, h:h + 1]
        b_row = b_rows[ML_HEADS + h:ML_HEADS + h + 1, :]
        i_row = gt[h:h + 1, :]

        d_log = jnp.where(causal, b_col - b_row + i_row, -jnp.inf)
        inter_log = b_col + m_prev
        m_row = jnp.maximum(inter_log, jnp.max(d_log, axis=-1, keepdims=True))
        w_intra = jnp.exp(d_log - m_row)
        w_inter = jnp.exp(inter_log - m_row)

        s = lax.dot_general(q, k, (((1,), (1,)), ((), ())), preferred_element_type=F32) * w_intra
        inter = lax.dot_general(q, c_prev.astype(BF16), (((1,), (1,)), ((), ())), preferred_element_type=F32)
        num = w_inter * inter + jnp.dot(s.astype(BF16), v, preferred_element_type=F32)
        den = w_inter * jnp.sum(q.astype(F32) * n_prev, axis=-1, keepdims=True) + jnp.sum(s, axis=-1, keepdims=True)
        hid = num / jnp.maximum(jnp.abs(den), jnp.exp(-m_row))

        m_new = m_row[chunk - 1:chunk, :]
        b_last = b_col[chunk - 1:chunk, :]
        w_prev = jnp.exp(b_last + m_prev - m_new)
        w_tok = jnp.exp(b_last - b_col + i_col - m_new)
        vw = (v.astype(F32) * w_tok).astype(BF16)
        c_sc[h] = w_prev * c_prev + lax.dot_general(vw, k, (((0,), (0,)), ((), ())), preferred_element_type=F32)
        n_sc[h:h + 1, :] = w_prev * n_prev + jnp.sum(k.astype(F32) * w_tok, axis=0, keepdims=True)
        m_sc[h:h + 1, :] = jnp.broadcast_to(m_new, (1, m_sc.shape[1]))

        normed = _head_rmsnorm(hid, gain_ref[h:h + 1, :])
        out_ref[:, vv] = (normed * _sigmoid(o_ref[:, vv])).astype(out_ref.dtype)

    @pl.when(ci == nci - 1)
    def _():
        c_out_ref[...] = c_sc[...]
        n_out_ref[...] = n_sc[...]
        m_out_ref[...] = m_sc[...]


def _mlstm(q, k, v, o, g, gt, gain, c0, n0, m0, mixed, *, batch, seq, chunk, row_offset, col_offset):
    nci = seq // chunk
    roff = row_offset // chunk
    vw = ML_HEADS * ML_V_DIM
    qw = ML_HEADS * ML_QK_DIM
    coff = col_offset // vw
    state_bytes = (_nbytes((ML_HEADS, ML_V_DIM, ML_QK_DIM), F32) + _nbytes((ML_HEADS, ML_QK_DIM), F32)
                   + _nbytes((ML_HEADS, V7X_LANES), F32))
    blocks = (2 * _nbytes((chunk, qw), BF16) + 2 * _nbytes((chunk, vw), BF16) + _nbytes((chunk, vw), F32)
              + _nbytes((chunk, V7X_LANES), F32) + 2 * state_bytes)
    row_map = lambda b, c: (b * nci + c + roff, 0)
    state_specs = [pl.BlockSpec((None, ML_HEADS, ML_V_DIM, ML_QK_DIM), lambda b, c: (b, 0, 0, 0)),
                   pl.BlockSpec((None, ML_HEADS, ML_QK_DIM), lambda b, c: (b, 0, 0)),
                   pl.BlockSpec((None, ML_HEADS, V7X_LANES), lambda b, c: (b, 0, 0))]
    return pl.pallas_call(
        functools.partial(_mlstm_kernel, chunk=chunk),
        out_shape=[jax.ShapeDtypeStruct(mixed.shape, mixed.dtype),
                   jax.ShapeDtypeStruct(c0.shape, F32),
                   jax.ShapeDtypeStruct(n0.shape, F32),
                   jax.ShapeDtypeStruct(m0.shape, F32)],
        grid=(batch, nci),
        in_specs=[pl.BlockSpec((chunk, qw), row_map),
                  pl.BlockSpec((chunk, qw), row_map),
                  pl.BlockSpec((chunk, vw), row_map),
                  pl.BlockSpec((chunk, vw), row_map),
                  pl.BlockSpec((chunk, V7X_LANES), lambda b, c: (b * nci + c, 0)),
                  pl.BlockSpec((None, N_GATES, chunk), lambda b, c: (b * nci + c, 0, 0)),
                  pl.BlockSpec((ML_HEADS, ML_V_DIM), lambda b, c: (0, 0)),
                  *state_specs,
                  pl.BlockSpec(memory_space=pl.ANY)],
        out_specs=[pl.BlockSpec((chunk, vw), lambda b, c: (b * nci + c + roff, coff)), *state_specs],
        scratch_shapes=[pltpu.VMEM((ML_HEADS, ML_V_DIM, ML_QK_DIM), F32),
                        pltpu.VMEM((ML_HEADS, ML_QK_DIM), F32),
                        pltpu.VMEM((ML_HEADS, V7X_LANES), F32)],
        input_output_aliases={10: 0},
        compiler_params=pltpu.CompilerParams(
            dimension_semantics=("parallel", "arbitrary"),
            vmem_limit_bytes=_vmem_limit(blocks, scratch_bytes=state_bytes, temp_bytes=12 << 20)),
        name="mlstm",
    )(q, k, v, o, g, gt, gain, c0, n0, m0, mixed)


def _pick_row_tile(t, target):
    best = 16
    for cand in range(16, target + 1, 16):
        if t % cand == 0:
            best = cand
    return best


def kernel(x_prompt, x_sample, cache_sb_k, cache_sb_v, state_ml_c, state_ml_n, state_ml_m, p_prompt, p_sample, ln_ffn1, w_ffn1_gate, w_ffn1_up, w_ffn1_down, ln_mix, w_in, b_if, g_sb_head, g_ml_head, w_out, ln_ffn2, w_ffn2_gate, w_ffn2_up, w_ffn2_down, ln_ple, w_ple_gate, w_ple_proj, ln_final):
    depth = w_in.shape[0]
    batch, seq, d_model = x_prompt.shape
    dec_batch, dec_seq, _ = x_sample.shape
    tp = batch * seq
    ts = dec_batch * dec_seq
    t_all = tp + ts
    sb_heads = g_sb_head.shape[1]
    sb_width = sb_heads * SB_HEAD_DIM
    ml_qk_width = ML_HEADS * ML_QK_DIM
    ml_v_width = ML_HEADS * ML_V_DIM
    d_ff = w_ffn1_gate.shape[2]
    off_q, off_k, off_v = 0, sb_width, 2 * sb_width
    off_mq = 3 * sb_width
    off_mk = off_mq + ml_qk_width
    off_mv = off_mk + ml_qk_width
    off_mo = off_mv + ml_v_width
    off_g = off_mo + ml_v_width

    tm = _pick_row_tile(t_all, 1088)
    norm_tile = _pick_row_tile(t_all, 384)
    ff_tn = V7X_MXU_DIM
    sb_tile = V7X_MXU_DIM
    ml_chunk = min(seq, V7X_MXU_DIM)

    x = jnp.concatenate([x_prompt.reshape(tp, d_model), x_sample.reshape(ts, d_model)], axis=0)

    def half_ffn(x, ln, w_gate, w_up, w_down):
        h = _rmsnorm(x, ln, BF16, rows=t_all, row_tile=norm_tile)
        a = _matmul_fullk(h, [w_gate, w_up], [0, 0], d_ff, [BF16], _ep_swiglu, tm=tm, tn=ff_tn, name="ffn_gate_up")[0]
        return _ffn_down(a, w_down, x, tm=tm, tn=1024, tk=1024)

    outs = {name: [] for name in ("pk", "pv", "pc", "pn", "pm", "sk", "sv", "sc", "sn", "sm")}
    for l in range(depth):
        x = half_ffn(x, ln_ffn1[l], w_ffn1_gate[l], w_ffn1_up[l], w_ffn1_down[l])

        h = _rmsnorm(x, ln_mix[l], BF16, rows=t_all, row_tile=norm_tile)
        w = w_in[l]
        (sb_q,) = _matmul_fullk(h, [w], [off_q], sb_width, [BF16], _ep_identity, tm=tm, tn=512, name="proj_sb_q")
        k_p, v_p = _matmul_fullk(h, [w, w], [off_k, off_v], sb_width, [F32, F32], _ep_identity,
                                 tm=1024, tn=256, rows=tp, name="proj_sb_kv_prompt")
        k_s, v_s = _matmul_fullk(h, [w, w], [off_k, off_v], sb_width, [F32, F32], _ep_identity,
                                 tm=ts, tn=256, rows=ts, row_offset=tp, name="proj_sb_kv_sample")
        ml_q, ml_k = _matmul_fullk(h, [w, w], [off_mq, off_mk], ml_qk_width, [BF16, BF16], _ep_ml_qk,
                                   tm=tm, tn=256, name="proj_ml_qk")
        ml_v, ml_o = _matmul_fullk(h, [w, w], [off_mv, off_mo], ml_v_width, [BF16, F32], _ep_identity,
                                   tm=tm, tn=256, name="proj_ml_vo")
        w_g = w[:, off_g:off_g + N_GATES]
        w_g_pad = jnp.pad(w_g, ((0, 0), (0, V7X_LANES - N_GATES)))
        b_row = jnp.pad(b_if[l], (0, V7X_LANES - N_GATES)).reshape(1, V7X_LANES)
        b_col = jnp.broadcast_to(b_if[l].reshape(N_GATES, 1), (N_GATES, V7X_LANES))
        g_p, gt_p = _gates(h, w_g_pad, w_g.T, b_row, b_col, rows=tp, chunk=ml_chunk, row_offset=0)
        g_s, gt_s = _gates(h, w_g_pad, w_g.T, b_row, b_col, rows=ts, chunk=dec_seq, row_offset=tp)

        mixed = jnp.zeros((t_all, sb_width + ml_v_width), BF16)
        mixed = _sb_prompt(sb_q, k_p, v_p, g_sb_head[l], mixed, batch=batch, seq=seq, tile=sb_tile)
        mixed = _sb_sample(sb_q, k_s, v_s, cache_sb_k[l].reshape(dec_batch, -1, sb_width),
                           cache_sb_v[l].reshape(dec_batch, -1, sb_width), g_sb_head[l], mixed,
                           row_offset=tp, tile=sb_tile)
        c0 = jnp.zeros((batch, ML_HEADS, ML_V_DIM, ML_QK_DIM), F32)
        n0 = jnp.zeros((batch, ML_HEADS, ML_QK_DIM), F32)
        m0 = jnp.zeros((batch, ML_HEADS, V7X_LANES), F32)
        mixed, c_p, n_p, m_p = _mlstm(ml_q, ml_k, ml_v, ml_o, g_p, gt_p, g_ml_head[l], c0, n0, m0, mixed,
                                      batch=batch, seq=seq, chunk=ml_chunk, row_offset=0, col_offset=sb_width)
        m0_s = jnp.broadcast_to(state_ml_m[l][:, :, None], (dec_batch, ML_HEADS, V7X_LANES))
        mixed, c_s, n_s, m_s = _mlstm(ml_q, ml_k, ml_v, ml_o, g_s, gt_s, g_ml_head[l], state_ml_c[l], state_ml_n[l],
                                      m0_s, mixed, batch=dec_batch, seq=dec_seq, chunk=dec_seq,
                                      row_offset=tp, col_offset=sb_width)

        (x,) = _matmul_fullk(mixed, [w_out[l]], [0], d_model, [F32], _ep_residual, extras=(x,),
                             tm=tm, tn=512, name="out_proj")
        x = half_ffn(x, ln_ffn2[l], w_ffn2_gate[l], w_ffn2_up[l], w_ffn2_down[l])

        h = _rmsnorm(x, ln_ple[l], BF16, rows=t_all, row_tile=norm_tile)
        p = jnp.concatenate([p_prompt[l].reshape(tp, -1), p_sample[l].reshape(ts, -1)], axis=0)
        x = _ple(h, w_ple_gate[l], p, w_ple_proj[l], x, tm=tm, tn=512)

        outs["pk"].append(k_p.reshape(batch, seq, sb_heads, SB_HEAD_DIM))
        outs["pv"].append(v_p.reshape(batch, seq, sb_heads, SB_HEAD_DIM))
        outs["pc"].append(c_p)
        outs["pn"].append(n_p)
        outs["pm"].append(m_p[:, :, 0])
        outs["sk"].append(k_s.reshape(dec_batch, dec_seq, sb_heads, SB_HEAD_DIM))
        outs["sv"].append(v_s.reshape(dec_batch, dec_seq, sb_heads, SB_HEAD_DIM))
        outs["sc"].append(c_s)
        outs["sn"].append(n_s)
        outs["sm"].append(m_s[:, :, 0])

    y_prompt = _rmsnorm(x, ln_final, F32, rows=tp, row_tile=256).reshape(batch, seq, d_model)
    y_sample = _rmsnorm(x, ln_final, F32, rows=ts, row_tile=ts, row_offset=tp).reshape(dec_batch, dec_seq, d_model)

    def stack(name):
        vals = outs[name]
        return vals[0][None] if len(vals) == 1 else jnp.stack(vals)

    return (y_prompt, y_sample, stack("pk"), stack("pv"), stack("pc"), stack("pn"), stack("pm"),
            stack("sk"), stack("sv"), stack("sc"), stack("sn"), stack("sm"))
```

```python
import functools
import math

import jax
import jax.numpy as jnp
from jax import lax
from jax.experimental import pallas as pl
from jax.experimental.pallas import tpu as pltpu

F32 = jnp.float32
BF16 = jnp.bfloat16
EPS = 1e-6
LOG2_E = math.log2(math.e)

V7X_VMEM_BYTES = 64 * 1024 * 1024
V7X_LANES = 128
V7X_MXU_DIM = 256
VMEM_REQUEST_CAP = V7X_VMEM_BYTES - 6 * 1024 * 1024

SB_HEAD_DIM = 128
ML_QK_DIM = 256
ML_V_DIM = 512
ML_HEADS = 4
N_GATES = 2 * ML_HEADS


def _vmem_limit(block_bytes, scratch_bytes=0, temp_bytes=0):
    est = 2 * block_bytes + scratch_bytes + temp_bytes + (2 << 20)
    return int(min(max(est, 16 << 20), VMEM_REQUEST_CAP))


def _nbytes(shape, dtype):
    n = 1
    for s in shape:
        n *= s
    return n * jnp.dtype(dtype).itemsize


def _sigmoid(x):
    return 1.0 / (1.0 + jnp.exp(-x))


def _rmsnorm_kernel(x_ref, g_ref, o_ref):
    x = x_ref[...]
    y = x * lax.rsqrt(jnp.mean(x * x, axis=-1, keepdims=True) + EPS)
    o_ref[...] = (y * g_ref[...]).astype(o_ref.dtype)


def _rmsnorm(x, g, out_dtype, *, rows, row_tile, row_offset=0):
    d = x.shape[1]
    off = row_offset // row_tile
    blocks = _nbytes((row_tile, d), F32) + _nbytes((row_tile, d), out_dtype)
    return pl.pallas_call(
        _rmsnorm_kernel,
        out_shape=jax.ShapeDtypeStruct((rows, d), out_dtype),
        grid=(rows // row_tile,),
        in_specs=[pl.BlockSpec((row_tile, d), lambda i: (i + off, 0)),
                  pl.BlockSpec((1, d), lambda i: (0, 0))],
        out_specs=pl.BlockSpec((row_tile, d), lambda i: (i, 0)),
        compiler_params=pltpu.CompilerParams(
            dimension_semantics=("parallel",),
            vmem_limit_bytes=_vmem_limit(blocks, temp_bytes=2 * _nbytes((row_tile, d), F32))),
        name="rmsnorm",
    )(x, g.reshape(1, d))


def _mm_kernel(*refs, n_w, n_extra, epilogue, w_transposed):
    lhs_ref = refs[0]
    w_refs = refs[1:1 + n_w]
    extra_refs = refs[1 + n_w:1 + n_w + n_extra]
    out_refs = refs[1 + n_w + n_extra:]
    lhs = lhs_ref[...]
    w_contract = 1 if w_transposed else 0
    accs = [lax.dot_general(lhs, w[...].astype(BF16), (((1,), (w_contract,)), ((), ())),
                            preferred_element_type=F32) for w in w_refs]
    outs = epilogue(accs, [e[...] for e in extra_refs])
    for o_ref, o in zip(out_refs, outs):
        o_ref[...] = o.astype(o_ref.dtype)


def _matmul_fullk(lhs, ws, col_offsets, n_cols, out_dtypes, epilogue, extras=(), *,
                  tm, tn, rows=None, row_offset=0, w_transposed=False, name):
    k_dim = lhs.shape[1]
    rows = lhs.shape[0] if rows is None else rows
    roff = row_offset // tm
    in_specs = [pl.BlockSpec((tm, k_dim), lambda i, j: (i + roff, 0))]
    for off in col_offsets:
        if w_transposed:
            in_specs.append(pl.BlockSpec((tn, k_dim), lambda i, j, o=off // tn: (j + o, 0)))
        else:
            in_specs.append(pl.BlockSpec((k_dim, tn), lambda i, j, o=off // tn: (0, j + o)))
    extra_bytes = 0
    for e in extras:
        if e.shape[0] == 1:
            in_specs.append(pl.BlockSpec((1, tn), lambda i, j: (0, j)))
            extra_bytes += _nbytes((8, tn), F32)
        else:
            in_specs.append(pl.BlockSpec((tm, tn), lambda i, j: (i + roff, j)))
            extra_bytes += _nbytes((tm, tn), F32)
    out_specs = [pl.BlockSpec((tm, tn), lambda i, j: (i, j)) for _ in out_dtypes]
    out_shape = [jax.ShapeDtypeStruct((rows, n_cols), dt) for dt in out_dtypes]
    blocks = (_nbytes((tm, k_dim), BF16) + len(ws) * _nbytes((k_dim, tn), F32) + extra_bytes
              + sum(_nbytes((tm, tn), dt) for dt in out_dtypes))
    temps = len(ws) * (_nbytes((k_dim, tn), BF16) + 2 * _nbytes((tm, tn), F32))
    return pl.pallas_call(
        functools.partial(_mm_kernel, n_w=len(ws), n_extra=len(extras), epilogue=epilogue,
                          w_transposed=w_transposed),
        out_shape=out_shape,
        grid=(rows // tm, n_cols // tn),
        in_specs=in_specs,
        out_specs=out_specs,
        compiler_params=pltpu.CompilerParams(
            dimension_semantics=("parallel", "arbitrary"),
            vmem_limit_bytes=_vmem_limit(blocks, temp_bytes=temps)),
        name=name,
    )(lhs, *ws, *extras)


def _ep_half_swiglu(accs, extras):
    g, u = accs
    return [0.5 * (g * _sigmoid(g) * u)]


def _ep_identity(accs, extras):
    return list(accs)


def _ep_kv(accs, extras):
    k, v = accs
    return [k, v, k, v]


def _ep_residual(accs, extras):
    return [extras[0] + accs[0]]


def _ep_bias(accs, extras):
    return [accs[0] + extras[0]]


def _ep_ml_qk(accs, extras):
    q, k = accs
    return [q * (ML_QK_DIM ** -0.5), k]


def _down_kernel(a_ref, w_ref, x_ref, o_ref, *, nk, k_valid_last, tk):
    k = pl.program_id(2)

    def partial_product(masked):
        a = a_ref[...]
        w = w_ref[...]
        if masked:
            a = jnp.where(lax.broadcasted_iota(jnp.int32, a.shape, 1) < k_valid_last, a, jnp.zeros_like(a))
            w = jnp.where(lax.broadcasted_iota(jnp.int32, w.shape, 0) < k_valid_last, w, jnp.zeros_like(w))
        return jnp.dot(a, w.astype(BF16), preferred_element_type=F32)

    @pl.when(k == 0)
    def _():
        o_ref[...] = x_ref[...] + partial_product(nk == 1 and k_valid_last != tk)

    if nk > 1:
        if k_valid_last == tk:
            @pl.when(k > 0)
            def _():
                o_ref[...] += partial_product(False)
        else:
            @pl.when(jnp.logical_and(k > 0, k < nk - 1))
            def _():
                o_ref[...] += partial_product(False)

            @pl.when(k == nk - 1)
            def _():
                o_ref[...] += partial_product(True)


def _ffn_down(a, w_down, x, *, tm, tn, tk):
    t, f = a.shape
    d = w_down.shape[1]
    nk = pl.cdiv(f, tk)
    k_valid_last = f - (nk - 1) * tk
    blocks = (_nbytes((tm, tk), BF16) + _nbytes((tk, tn), F32) + 2 * _nbytes((tm, tn), F32))
    return pl.pallas_call(
        functools.partial(_down_kernel, nk=nk, k_valid_last=k_valid_last, tk=tk),
        out_shape=jax.ShapeDtypeStruct((t, d), F32),
        grid=(t // tm, d // tn, nk),
        in_specs=[pl.BlockSpec((tm, tk), lambda i, j, k: (i, k)),
                  pl.BlockSpec((tk, tn), lambda i, j, k: (k, j)),
                  pl.BlockSpec((tm, tn), lambda i, j, k: (i, j))],
        out_specs=pl.BlockSpec((tm, tn), lambda i, j, k: (i, j)),
        compiler_params=pltpu.CompilerParams(
            dimension_semantics=("parallel", "parallel", "arbitrary"),
            vmem_limit_bytes=_vmem_limit(blocks, temp_bytes=_nbytes((tk, tn), BF16) + (4 << 20))),
        name="ffn_down",
    )(a, w_down, x)


def _ple_kernel(h_ref, wg_ref, p_ref, wp_ref, x_ref, o_ref):
    gate = _sigmoid(jnp.dot(h_ref[...], wg_ref[...].astype(BF16), preferred_element_type=F32))
    proj = jnp.dot(p_ref[...].astype(BF16), wp_ref[...].astype(BF16), preferred_element_type=F32)
    o_ref[...] = x_ref[...] + gate * proj


def _ple(h, w_gate, p, w_proj, x, *, tm, tn):
    t, d = h.shape
    pd = p.shape[1]
    blocks = (_nbytes((tm, d), BF16) + _nbytes((d, tn), F32) + _nbytes((tm, pd), F32)
              + _nbytes((pd, tn), F32) + 2 * _nbytes((tm, tn), F32))
    return pl.pallas_call(
        _ple_kernel,
        out_shape=jax.ShapeDtypeStruct((t, d), F32),
        grid=(t // tm, d // tn),
        in_specs=[pl.BlockSpec((tm, d), lambda i, j: (i, 0)),
                  pl.BlockSpec((d, tn), lambda i, j: (0, j)),
                  pl.BlockSpec((tm, pd), lambda i, j: (i, 0)),
                  pl.BlockSpec((pd, tn), lambda i, j: (0, j)),
                  pl.BlockSpec((tm, tn), lambda i, j: (i, j))],
        out_specs=pl.BlockSpec((tm, tn), lambda i, j: (i, j)),
        compiler_params=pltpu.CompilerParams(
            dimension_semantics=("parallel", "arbitrary"),
            vmem_limit_bytes=_vmem_limit(blocks, temp_bytes=_nbytes((d, tn), BF16) + 3 * _nbytes((tm, tn), F32))),
        name="ple",
    )(h, w_gate, p, w_proj, x)


def _split3(x):
    hi = x.astype(BF16)
    r = x - hi.astype(F32)
    mid = r.astype(BF16)
    lo = (r - mid.astype(F32)).astype(BF16)
    return hi, mid, lo


def _neg_inclusive_matrix(tk):
    r = lax.broadcasted_iota(jnp.int32, (2 * tk, tk), 0)
    c = lax.broadcasted_iota(jnp.int32, (2 * tk, tk), 1)
    r = jnp.where(r >= tk, r - tk, r)
    return jnp.where(r >= c, -1.0, 0.0).astype(BF16)


def _sb_scores(q, kblk, *, scale2, diagonal):
    z2 = lax.dot_general(q, kblk, (((2,), (2,)), ((0,), (0,))), preferred_element_type=F32) * scale2
    heads, tq, tk = z2.shape
    sp2 = jnp.maximum(z2, 0.0) + jnp.log2(1.0 + jnp.exp2(-jnp.abs(z2)))
    if diagonal:
        row = lax.broadcasted_iota(jnp.int32, (1, tq, tk), 1)
        col = lax.broadcasted_iota(jnp.int32, (1, tq, tk), 2)
        mask = col < row
        sp2 = jnp.where(mask, sp2, 0.0)
        z2 = jnp.where(mask, z2, -jnp.inf)
    hi = sp2.astype(BF16)
    lo = (sp2 - hi.astype(F32)).astype(BF16)
    parts = jnp.concatenate([hi, lo], axis=-1).reshape(heads * tq, 2 * tk)
    return z2, parts


def _sb_accumulate(z2, parts, vblk, neg_incl, run, acc):
    heads, tq, tk = z2.shape
    incl = jnp.dot(parts, neg_incl, preferred_element_type=F32).reshape(heads, tq, tk)
    a = jnp.exp2((z2 + run) + incl)
    acc = acc + lax.dot_general(a.astype(BF16), vblk, (((2,), (1,)), ((0,), (0,))), preferred_element_type=F32)
    run = run + incl[:, :, 0:1]
    return run, acc


def _head_rmsnorm(x, g):
    return x * lax.rsqrt(jnp.mean(x * x, axis=-1, keepdims=True) + EPS) * g


def _sb_prompt_kernel(q_ref, k_ref, v_ref, g_ref, o_ref, *, tile, heads, scale2):
    qi = pl.program_id(2)
    d = SB_HEAD_DIM
    neg_incl = _neg_inclusive_matrix(tile)
    q = jnp.stack([q_ref[:, h * d:(h + 1) * d] for h in range(heads)])

    def key_rows(kb):
        return pl.ds(pl.multiple_of(kb * tile, tile), tile)

    def visit(kb, carry, diagonal):
        kblk = jnp.stack([k_ref[key_rows(kb), h * d:(h + 1) * d] for h in range(heads)])
        vblk = jnp.stack([v_ref[key_rows(kb), h * d:(h + 1) * d] for h in range(heads)])
        z2, parts = _sb_scores(q, kblk, scale2=scale2, diagonal=diagonal)
        return _sb_accumulate(z2, parts, vblk, neg_incl, *carry)

    carry = (jnp.zeros((heads, tile, 1), F32), jnp.zeros((heads, tile, d), F32))
    carry = visit(qi, carry, True)
    carry = lax.fori_loop(0, qi, lambda t, c: visit(qi - 1 - t, c, False), carry)
    for h in range(heads):
        cols = slice(h * d, (h + 1) * d)
        o_ref[:, cols] = _head_rmsnorm(carry[1][h], g_ref[:, cols]).astype(o_ref.dtype)


def _sb_prompt(q, k, v, g_heads, *, t_all, mix_width, batch, seq, tile, heads_per_step):
    d = SB_HEAD_DIM
    w = heads_per_step * d
    groups = k.shape[1] // w
    nq = seq // tile
    blocks = (2 * _nbytes((tile, w), BF16) + 2 * _nbytes((seq, w), BF16) + _nbytes((8, w), F32))
    return pl.pallas_call(
        functools.partial(_sb_prompt_kernel, tile=tile, heads=heads_per_step, scale2=LOG2_E * d ** -0.5),
        out_shape=jax.ShapeDtypeStruct((t_all, mix_width), BF16),
        grid=(batch, groups, nq),
        in_specs=[pl.BlockSpec((tile, w), lambda b, h, i: (b * nq + i, h)),
                  pl.BlockSpec((seq, w), lambda b, h, i: (b, h)),
                  pl.BlockSpec((seq, w), lambda b, h, i: (b, h)),
                  pl.BlockSpec((None, 1, w), lambda b, h, i: (h, 0, 0))],
        out_specs=pl.BlockSpec((tile, w), lambda b, h, i: (b * nq + i, h)),
        compiler_params=pltpu.CompilerParams(
            dimension_semantics=("parallel", "parallel", "arbitrary"),
            vmem_limit_bytes=_vmem_limit(blocks, temp_bytes=16 * heads_per_step * _nbytes((tile, tile), F32))),
        name="sb_prompt",
    )(q, k, v, g_heads.reshape(groups, 1, w))


def _sb_sample_kernel(q_ref, kn_ref, vn_ref, kc_ref, vc_ref, g_ref, mixed_ref, o_ref, run_ref, acc_ref,
                      *, heads, tile, scale2):
    del mixed_ref
    kc = pl.program_id(1)
    nkc = pl.num_programs(1)
    d = SB_HEAD_DIM
    n_new = kn_ref.shape[0]
    q = jnp.stack([q_ref[:, h * d:(h + 1) * d] for h in range(heads)])

    @pl.when(kc == 0)
    def _():
        pad = jnp.zeros((heads, V7X_LANES - n_new, d), BF16)
        kblk = jnp.concatenate([jnp.stack([kn_ref[:, h * d:(h + 1) * d] for h in range(heads)]).astype(BF16), pad], axis=1)
        vblk = jnp.concatenate([jnp.stack([vn_ref[:, h * d:(h + 1) * d] for h in range(heads)]).astype(BF16), pad], axis=1)
        z2, parts = _sb_scores(q, kblk, scale2=scale2, diagonal=True)
        run, acc = _sb_accumulate(z2, parts, vblk, _neg_inclusive_matrix(V7X_LANES),
                                  jnp.zeros((heads, n_new, 1), F32), jnp.zeros((heads, n_new, d), F32))
        run_ref[...] = run
        acc_ref[...] = acc

    kblk = jnp.stack([kc_ref[pl.ds(h, tile, stride=heads), :] for h in range(heads)]).astype(BF16)
    vblk = jnp.stack([vc_ref[pl.ds(h, tile, stride=heads), :] for h in range(heads)]).astype(BF16)
    z2, parts = _sb_scores(q, kblk, scale2=scale2, diagonal=False)
    run, acc = _sb_accumulate(z2, parts, vblk, _neg_inclusive_matrix(tile), run_ref[...], acc_ref[...])
    run_ref[...] = run
    acc_ref[...] = acc

    @pl.when(kc == nkc - 1)
    def _():
        for h in range(heads):
            cols = slice(h * d, (h + 1) * d)
            o_ref[:, cols] = _head_rmsnorm(acc_ref[h], g_ref[h:h + 1, :]).astype(o_ref.dtype)


def _sb_sample(q, k_new, v_new, k_cache, v_cache, g_heads, mixed, *, layer, row_offset, tile):
    _, batch, past, heads, d = k_cache.shape
    width = heads * d
    n_new = k_new.shape[0] // batch
    roff = row_offset // n_new
    nkc = past // tile
    blocks = (2 * _nbytes((n_new, width), BF16) + 2 * _nbytes((n_new, width), F32)
              + 2 * _nbytes((tile, heads, d), F32) + _nbytes((heads, d), F32))
    state_bytes = _nbytes((heads, n_new, V7X_LANES), F32) + _nbytes((heads, n_new, d), F32)
    k_cache = k_cache.reshape(-1, batch, past * heads, d)
    v_cache = v_cache.reshape(-1, batch, past * heads, d)
    cache_spec = pl.BlockSpec((None, None, tile * heads, d), lambda b, c: (layer, b, nkc - 1 - c, 0))
    return pl.pallas_call(
        functools.partial(_sb_sample_kernel, heads=heads, tile=tile, scale2=LOG2_E * d ** -0.5),
        out_shape=jax.ShapeDtypeStruct(mixed.shape, mixed.dtype),
        grid=(batch, nkc),
        in_specs=[pl.BlockSpec((n_new, width), lambda b, c: (b + roff, 0)),
                  pl.BlockSpec((n_new, width), lambda b, c: (b, 0)),
                  pl.BlockSpec((n_new, width), lambda b, c: (b, 0)),
                  cache_spec,
                  cache_spec,
                  pl.BlockSpec((heads, d), lambda b, c: (0, 0)),
                  pl.BlockSpec(memory_space=pl.ANY)],
        out_specs=pl.BlockSpec((n_new, width), lambda b, c: (b + roff, 0)),
        scratch_shapes=[pltpu.VMEM((heads, n_new, 1), F32), pltpu.VMEM((heads, n_new, d), F32)],
        input_output_aliases={6: 0},
        compiler_params=pltpu.CompilerParams(
            dimension_semantics=("parallel", "arbitrary"),
            vmem_limit_bytes=_vmem_limit(blocks, scratch_bytes=state_bytes, temp_bytes=12 << 20)),
        name="sb_sample",
    )(q, k_new, v_new, k_cache, v_cache, g_heads, mixed)


def _log_sigmoid(x):
    return jnp.minimum(x, 0.0) - jnp.log1p(jnp.exp(-jnp.abs(x)))


def _gate_rows(x, chunk):
    pad = (-chunk) % V7X_LANES
    if pad:
        x = jnp.concatenate([x, jnp.zeros((pad, V7X_LANES), x.dtype)], axis=0)
    return x.T[:N_GATES, :chunk]


def _mlstm_kernel(q_ref, k_ref, v_ref, o_ref, g_ref, gain_ref, c0_ref, n0_ref, m0_ref, mixed_ref,
                  out_ref, c_out_ref, n_out_ref, m_out_ref, c_sc, n_sc, m_sc, *, chunk):
    del mixed_ref
    ci = pl.program_id(1)
    nci = pl.num_programs(1)

    @pl.when(ci == 0)
    def _():
        c_sc[...] = c0_ref[...]
        n_sc[...] = n0_ref[...]
        m_sc[...] = m0_ref[...]

    row = lax.broadcasted_iota(jnp.int32, (chunk, chunk), 0)
    col = lax.broadcasted_iota(jnp.int32, (chunk, chunk), 1)
    causal = col <= row
    tri = causal.astype(BF16)

    g = g_ref[...]
    b_cols = sum(jnp.dot(tri, part, preferred_element_type=F32) for part in _split3(_log_sigmoid(g)))
    g_rows = _gate_rows(g, chunk)
    b_rows = _gate_rows(b_cols, chunk)

    for h in range(ML_HEADS):
        qk = slice(h * ML_QK_DIM, (h + 1) * ML_QK_DIM)
        vv = slice(h * ML_V_DIM, (h + 1) * ML_V_DIM)
        q = q_ref[:, qk]
        k = k_ref[:, qk]
        v = v_ref[:, vv]
        c_prev = c_sc[h]
        n_prev = n_sc[h:h + 1, :]
        m_prev = m_sc[h:h + 1, 0:1]

        b_col = b_cols[:, ML_HEADS + h:ML_HEADS + h + 1]
        i_col = g[:, h:h + 1]
        b_row = b_rows[ML_HEADS + h:ML_HEADS + h + 1, :]
        i_row = g_rows[h:h + 1, :]

        d_log = jnp.where(causal, b_col - b_row + i_row, -jnp.inf)
        inter_log = b_col + m_prev
        m_row = jnp.maximum(inter_log, jnp.max(d_log, axis=-1, keepdims=True))
        w_intra = jnp.exp(d_log - m_row)
        w_inter = jnp.exp(inter_log - m_row)

        s = lax.dot_general(q, k, (((1,), (1,)), ((), ())), preferred_element_type=F32) * w_intra
        inter = lax.dot_general(q, c_prev.astype(BF16), (((1,), (1,)), ((), ())), preferred_element_type=F32)
        num = w_inter * inter + jnp.dot(s.astype(BF16), v, preferred_element_type=F32)
        den = w_inter * jnp.sum(q.astype(F32) * n_prev, axis=-1, keepdims=True) + jnp.sum(s, axis=-1, keepdims=True)
        hid = num / jnp.maximum(jnp.abs(den), jnp.exp(-m_row))

        m_new = m_row[chunk - 1:chunk, :]
        b_last = b_col[chunk - 1:chunk, :]
        w_prev = jnp.exp(b_last + m_prev - m_new)
        w_tok = jnp.exp(b_last - b_col + i_col - m_new)
        vw = (v.astype(F32) * w_tok).astype(BF16)
        c_sc[h] = w_prev * c_prev + lax.dot_general(vw, k, (((0,), (0,)), ((), ())), preferred_element_type=F32)
        n_sc[h:h + 1, :] = w_prev * n_prev + jnp.sum(k.astype(F32) * w_tok, axis=0, keepdims=True)
        m_sc[h:h + 1, :] = jnp.broadcast_to(m_new, (1, m_sc.shape[1]))

        normed = _head_rmsnorm(hid, gain_ref[h:h + 1, :])
        out_ref[:, vv] = (normed * _sigmoid(o_ref[:, vv])).astype(out_ref.dtype)

    @pl.when(ci == nci - 1)
    def _():
        c_out_ref[...] = c_sc[...]
        n_out_ref[...] = n_sc[...]
        m_out_ref[...] = m_sc[...]


def _mlstm(q, k, v, o, g, gain, c0, n0, m0, mixed, *, batch, seq, chunk, row_offset, col_offset):
    nci = seq // chunk
    roff = row_offset // chunk
    vw = ML_HEADS * ML_V_DIM
    qw = ML_HEADS * ML_QK_DIM
    coff = col_offset // vw
    state_bytes = (_nbytes((ML_HEADS, ML_V_DIM, ML_QK_DIM), F32) + _nbytes((ML_HEADS, ML_QK_DIM), F32)
                   + _nbytes((ML_HEADS, V7X_LANES), F32))
    blocks = (2 * _nbytes((chunk, qw), BF16) + 2 * _nbytes((chunk, vw), BF16) + _nbytes((chunk, vw), F32)
              + _nbytes((chunk, V7X_LANES), F32) + 2 * state_bytes)
    row_map = lambda b, c: (b * nci + c + roff, 0)
    state_specs = [pl.BlockSpec((None, ML_HEADS, ML_V_DIM, ML_QK_DIM), lambda b, c: (b, 0, 0, 0)),
                   pl.BlockSpec((None, ML_HEADS, ML_QK_DIM), lambda b, c: (b, 0, 0)),
                   pl.BlockSpec((None, ML_HEADS, V7X_LANES), lambda b, c: (b, 0, 0))]
    return pl.pallas_call(
        functools.partial(_mlstm_kernel, chunk=chunk),
        out_shape=[jax.ShapeDtypeStruct(mixed.shape, mixed.dtype),
                   jax.ShapeDtypeStruct(c0.shape, F32),
                   jax.ShapeDtypeStruct(n0.shape, F32),
                   jax.ShapeDtypeStruct(m0.shape, F32)],
        grid=(batch, nci),
        in_specs=[pl.BlockSpec((chunk, qw), row_map),
                  pl.BlockSpec((chunk, qw), row_map),
                  pl.BlockSpec((chunk, vw), row_map),
                  pl.BlockSpec((chunk, vw), row_map),
                  pl.BlockSpec((chunk, V7X_LANES), row_map),
                  pl.BlockSpec((ML_HEADS, ML_V_DIM), lambda b, c: (0, 0)),
                  *state_specs,
                  pl.BlockSpec(memory_space=pl.ANY)],
        out_specs=[pl.BlockSpec((chunk, vw), lambda b, c: (b * nci + c + roff, coff)), *state_specs],
        scratch_shapes=[pltpu.VMEM((ML_HEADS, ML_V_DIM, ML_QK_DIM), F32),
                        pltpu.VMEM((ML_HEADS, ML_QK_DIM), F32),
                        pltpu.VMEM((ML_HEADS, V7X_LANES), F32)],
        input_output_aliases={9: 0},
        compiler_params=pltpu.CompilerParams(
            dimension_semantics=("parallel", "arbitrary"),
            vmem_limit_bytes=_vmem_limit(blocks, scratch_bytes=state_bytes, temp_bytes=12 << 20)),
        name="mlstm",
    )(q, k, v, o, g, gain, c0, n0, m0, mixed)


def _pick_row_tile(t, target):
    best = 16
    for cand in range(16, target + 1, 16):
        if t % cand == 0:
            best = cand
    return best


def kernel(x_prompt, x_sample, cache_sb_k, cache_sb_v, state_ml_c, state_ml_n, state_ml_m, p_prompt, p_sample, ln_ffn1, w_ffn1_gate, w_ffn1_up, w_ffn1_down, ln_mix, w_in, b_if, g_sb_head, g_ml_head, w_out, ln_ffn2, w_ffn2_gate, w_ffn2_up, w_ffn2_down, ln_ple, w_ple_gate, w_ple_proj, ln_final):
    depth = w_in.shape[0]
    batch, seq, d_model = x_prompt.shape
    dec_batch, dec_seq, _ = x_sample.shape
    tp = batch * seq
    ts = dec_batch * dec_seq
    t_all = tp + ts
    sb_heads = g_sb_head.shape[1]
    sb_width = sb_heads * SB_HEAD_DIM
    ml_qk_width = ML_HEADS * ML_QK_DIM
    ml_v_width = ML_HEADS * ML_V_DIM
    mix_width = sb_width + ml_v_width
    d_ff = w_ffn1_gate.shape[2]
    off_q, off_k, off_v = 0, sb_width, 2 * sb_width
    off_mq = 3 * sb_width
    off_mk = off_mq + ml_qk_width
    off_mv = off_mk + ml_qk_width
    off_mo = off_mv + ml_v_width
    off_g = off_mo + ml_v_width

    tm = _pick_row_tile(t_all, 1088)
    tm_down = _pick_row_tile(t_all, 2176)
    norm_tile = _pick_row_tile(t_all, 384)
    ff_tn = V7X_MXU_DIM
    sb_tile = V7X_MXU_DIM
    ml_chunk = min(seq, V7X_MXU_DIM)

    x = jnp.concatenate([x_prompt.reshape(tp, d_model), x_sample.reshape(ts, d_model)], axis=0)

    def half_ffn(x, ln, w_gate, w_up, w_down):
        h = _rmsnorm(x, ln, BF16, rows=t_all, row_tile=norm_tile)
        a = _matmul_fullk(h, [w_gate, w_up], [0, 0], d_ff, [BF16], _ep_half_swiglu, tm=tm, tn=ff_tn,
                          name="ffn_gate_up")[0]
        return _ffn_down(a, w_down, x, tm=tm_down, tn=1024, tk=1024)

    outs = {name: [] for name in ("pk", "pv", "pc", "pn", "pm", "sk", "sv", "sc", "sn", "sm")}
    for l in range(depth):
        x = half_ffn(x, ln_ffn1[l], w_ffn1_gate[l], w_ffn1_up[l], w_ffn1_down[l])

        h = _rmsnorm(x, ln_mix[l], BF16, rows=t_all, row_tile=norm_tile)
        w = jnp.swapaxes(w_in[l], 0, 1)
        proj = functools.partial(_matmul_fullk, h, w_transposed=True)
        (sb_q,) = proj([w], [off_q], sb_width, [BF16], _ep_identity, tm=tm, tn=512, name="proj_sb_q")
        k_p, v_p, k_pb, v_pb = proj([w, w], [off_k, off_v], sb_width, [F32, F32, BF16, BF16], _ep_kv,
                                    tm=1024, tn=256, rows=tp, name="proj_sb_kv_prompt")
        k_s, v_s = proj([w, w], [off_k, off_v], sb_width, [F32, F32], _ep_identity,
                        tm=ts, tn=256, rows=ts, row_offset=tp, name="proj_sb_kv_sample")
        ml_q, ml_k = proj([w, w], [off_mq, off_mk], ml_qk_width, [BF16, BF16], _ep_ml_qk,
                          tm=tm, tn=256, name="proj_ml_qk")
        ml_v, ml_o = proj([w, w], [off_mv, off_mo], ml_v_width, [BF16, F32], _ep_identity,
                          tm=tm, tn=256, name="proj_ml_vo")
        b_row = jnp.pad(b_if[l], (0, V7X_LANES - N_GATES)).reshape(1, V7X_LANES)
        (gates,) = proj([w], [off_g], V7X_LANES, [F32], _ep_bias, extras=(b_row,),
                        tm=tm, tn=V7X_LANES, name="proj_ml_gates")

        mixed = _sb_prompt(sb_q, k_pb, v_pb, g_sb_head[l], t_all=t_all, mix_width=mix_width, batch=batch, seq=seq,
                           tile=sb_tile, heads_per_step=8)
        mixed = _sb_sample(sb_q, k_s, v_s, cache_sb_k, cache_sb_v, g_sb_head[l], mixed,
                           layer=l, row_offset=tp, tile=sb_tile)
        c0 = jnp.zeros((batch, ML_HEADS, ML_V_DIM, ML_QK_DIM), F32)
        n0 = jnp.zeros((batch, ML_HEADS, ML_QK_DIM), F32)
        m0 = jnp.zeros((batch, ML_HEADS, V7X_LANES), F32)
        mixed, c_p, n_p, m_p = _mlstm(ml_q, ml_k, ml_v, ml_o, gates, g_ml_head[l], c0, n0, m0, mixed,
                                      batch=batch, seq=seq, chunk=ml_chunk, row_offset=0, col_offset=sb_width)
        m0_s = jnp.broadcast_to(state_ml_m[l][:, :, None], (dec_batch, ML_HEADS, V7X_LANES))
        mixed, c_s, n_s, m_s = _mlstm(ml_q, ml_k, ml_v, ml_o, gates, g_ml_head[l], state_ml_c[l], state_ml_n[l],
                                      m0_s, mixed, batch=dec_batch, seq=dec_seq, chunk=dec_seq,
                                      row_offset=tp, col_offset=sb_width)

        (x,) = _matmul_fullk(mixed, [w_out[l]], [0], d_model, [F32], _ep_residual, extras=(x,),
                             tm=tm, tn=512, name="out_proj")
        x = half_ffn(x, ln_ffn2[l], w_ffn2_gate[l], w_ffn2_up[l], w_ffn2_down[l])

        h = _rmsnorm(x, ln_ple[l], BF16, rows=t_all, row_tile=norm_tile)
        p = jnp.concatenate([p_prompt[l].reshape(tp, -1), p_sample[l].reshape(ts, -1)], axis=0)
        x = _ple(h, w_ple_gate[l], p, w_ple_proj[l], x, tm=tm, tn=512)

        outs["pk"].append(k_p.reshape(batch, seq, sb_heads, SB_HEAD_DIM))
        outs["pv"].append(v_p.reshape(batch, seq, sb_heads, SB_HEAD_DIM))
        outs["pc"].append(c_p)
        outs["pn"].append(n_p)
        outs["pm"].append(m_p[:, :, 0])
        outs["sk"].append(k_s.reshape(dec_batch, dec_seq, sb_heads, SB_HEAD_DIM))
        outs["sv"].append(v_s.reshape(dec_batch, dec_seq, sb_heads, SB_HEAD_DIM))
        outs["sc"].append(c_s)
        outs["sn"].append(n_s)
        outs["sm"].append(m_s[:, :, 0])

    y_prompt = _rmsnorm(x, ln_final, F32, rows=tp, row_tile=256).reshape(batch, seq, d_model)
    y_sample = _rmsnorm(x, ln_final, F32, rows=ts, row_tile=ts, row_offset=tp).reshape(dec_batch, dec_seq, d_model)

    def stack(name):
        vals = outs[name]
        return vals[0][None] if len(vals) == 1 else jnp.stack(vals)

    return (y_prompt, y_sample, stack("pk"), stack("pv"), stack("pc"), stack("pn"), stack("pm"),
            stack("sk"), stack("sv"), stack("sc"), stack("sn"), stack("sm"))
```

```python
import functools
import math

import jax
import jax.numpy as jnp
from jax import lax
from jax.experimental import pallas as pl
from jax.experimental.pallas import tpu as pltpu

F32 = jnp.float32
BF16 = jnp.bfloat16
EPS = 1e-6
LOG2_E = math.log2(math.e)

V7X_VMEM_BYTES = 64 * 1024 * 1024
V7X_LANES = 128
V7X_MXU_DIM = 256
VMEM_REQUEST_CAP = V7X_VMEM_BYTES - 6 * 1024 * 1024

SB_HEAD_DIM = 128
ML_QK_DIM = 256
ML_V_DIM = 512
ML_HEADS = 4
N_GATES = 2 * ML_HEADS
SB_Q_SCALE = LOG2_E * SB_HEAD_DIM ** -0.5


def _vmem_limit(block_bytes, scratch_bytes=0, temp_bytes=0):
    est = 2 * block_bytes + scratch_bytes + temp_bytes + (2 << 20)
    return int(min(max(est, 16 << 20), VMEM_REQUEST_CAP))


def _nbytes(shape, dtype):
    n = 1
    for s in shape:
        n *= s
    return n * jnp.dtype(dtype).itemsize


def _sigmoid(x):
    return 1.0 / (1.0 + jnp.exp(-x))


def _rmsnorm_kernel(x_ref, g_ref, o_ref):
    x = x_ref[...]
    y = x * lax.rsqrt(jnp.mean(x * x, axis=-1, keepdims=True) + EPS)
    o_ref[...] = (y * g_ref[...]).astype(o_ref.dtype)


def _rmsnorm(x, g, out_dtype, *, rows, row_tile, row_offset=0):
    d = x.shape[1]
    off = row_offset // row_tile
    blocks = _nbytes((row_tile, d), F32) + _nbytes((row_tile, d), out_dtype)
    return pl.pallas_call(
        _rmsnorm_kernel,
        out_shape=jax.ShapeDtypeStruct((rows, d), out_dtype),
        grid=(rows // row_tile,),
        in_specs=[pl.BlockSpec((row_tile, d), lambda i: (i + off, 0)),
                  pl.BlockSpec((1, d), lambda i: (0, 0))],
        out_specs=pl.BlockSpec((row_tile, d), lambda i: (i, 0)),
        compiler_params=pltpu.CompilerParams(
            dimension_semantics=("parallel",),
            vmem_limit_bytes=_vmem_limit(blocks, temp_bytes=2 * _nbytes((row_tile, d), F32))),
        name="rmsnorm",
    )(x, g.reshape(1, d))


def _mm_kernel(*refs, n_w, n_extra, epilogue, w_transposed):
    lhs_ref = refs[0]
    w_refs = refs[1:1 + n_w]
    extra_refs = refs[1 + n_w:1 + n_w + n_extra]
    out_refs = refs[1 + n_w + n_extra:]
    lhs = lhs_ref[...]
    w_contract = 1 if w_transposed else 0
    accs = [lax.dot_general(lhs, w[...].astype(BF16), (((1,), (w_contract,)), ((), ())),
                            preferred_element_type=F32) for w in w_refs]
    outs = epilogue(accs, [e[...] for e in extra_refs])
    for o_ref, o in zip(out_refs, outs):
        o_ref[...] = o.astype(o_ref.dtype)


def _matmul_fullk(lhs, ws, col_offsets, n_cols, out_dtypes, epilogue, extras=(), *,
                  tm, tn, rows=None, row_offset=0, w_transposed=False, single_buffer_lhs=False, name):
    k_dim = lhs.shape[1]
    rows = lhs.shape[0] if rows is None else rows
    roff = row_offset // tm
    lhs_mode = dict(pipeline_mode=pl.Buffered(1)) if single_buffer_lhs else {}
    in_specs = [pl.BlockSpec((tm, k_dim), lambda i, j: (i + roff, 0), **lhs_mode)]
    for off in col_offsets:
        if w_transposed:
            in_specs.append(pl.BlockSpec((tn, k_dim), lambda i, j, o=off // tn: (j + o, 0)))
        else:
            in_specs.append(pl.BlockSpec((k_dim, tn), lambda i, j, o=off // tn: (0, j + o)))
    extra_bytes = 0
    for e in extras:
        if e.shape[0] == 1:
            in_specs.append(pl.BlockSpec((1, tn), lambda i, j: (0, j)))
            extra_bytes += _nbytes((8, tn), F32)
        else:
            in_specs.append(pl.BlockSpec((tm, tn), lambda i, j: (i + roff, j)))
            extra_bytes += _nbytes((tm, tn), F32)
    out_specs = [pl.BlockSpec((tm, tn), lambda i, j: (i, j)) for _ in out_dtypes]
    out_shape = [jax.ShapeDtypeStruct((rows, n_cols), dt) for dt in out_dtypes]
    blocks = (_nbytes((tm, k_dim), BF16) + len(ws) * _nbytes((k_dim, tn), F32) + extra_bytes
              + sum(_nbytes((tm, tn), dt) for dt in out_dtypes))
    temps = (len(ws) * ((3 if w_transposed else 1) * _nbytes((k_dim, tn), BF16) + 4 * _nbytes((tm, tn), F32))
             + len(extras) * _nbytes((tm, tn), F32))
    if single_buffer_lhs:
        temps -= _nbytes((tm, k_dim), BF16)
    return pl.pallas_call(
        functools.partial(_mm_kernel, n_w=len(ws), n_extra=len(extras), epilogue=epilogue,
                          w_transposed=w_transposed),
        out_shape=out_shape,
        grid=(rows // tm, n_cols // tn),
        in_specs=in_specs,
        out_specs=out_specs,
        compiler_params=pltpu.CompilerParams(
            dimension_semantics=("parallel", "arbitrary"),
            vmem_limit_bytes=_vmem_limit(blocks, temp_bytes=temps)),
        name=name,
    )(lhs, *ws, *extras)


def _ep_half_swiglu(accs, extras):
    g, u = accs
    return [0.5 * (g * _sigmoid(g) * u)]


def _ep_identity(accs, extras):
    return list(accs)


def _ep_kv(accs, extras):
    k, v = accs
    return [k, v, k, v]


def _ep_residual(accs, extras):
    return [extras[0] + accs[0]]


def _ep_bias(accs, extras):
    return [accs[0] + extras[0]]


def _ep_sb_q(accs, extras):
    return [accs[0] * SB_Q_SCALE]


def _ep_ml_qk(accs, extras):
    q, k = accs
    return [q * (ML_QK_DIM ** -0.5), k]


def _down_kernel(a_ref, w_ref, x_ref, o_ref, *, nk, k_valid_last, tk):
    k = pl.program_id(2)

    def partial_product(masked):
        a = a_ref[...]
        w = w_ref[...]
        if masked:
            a = jnp.where(lax.broadcasted_iota(jnp.int32, a.shape, 1) < k_valid_last, a, jnp.zeros_like(a))
            w = jnp.where(lax.broadcasted_iota(jnp.int32, w.shape, 0) < k_valid_last, w, jnp.zeros_like(w))
        return jnp.dot(a, w.astype(BF16), preferred_element_type=F32)

    @pl.when(k == 0)
    def _():
        o_ref[...] = x_ref[...] + partial_product(nk == 1 and k_valid_last != tk)

    if nk > 1:
        if k_valid_last == tk:
            @pl.when(k > 0)
            def _():
                o_ref[...] += partial_product(False)
        else:
            @pl.when(jnp.logical_and(k > 0, k < nk - 1))
            def _():
                o_ref[...] += partial_product(False)

            @pl.when(k == nk - 1)
            def _():
                o_ref[...] += partial_product(True)


def _ffn_down(a, w_down, x, *, tm, tn, tk):
    t, f = a.shape
    d = w_down.shape[1]
    nk = pl.cdiv(f, tk)
    k_valid_last = f - (nk - 1) * tk
    blocks = (_nbytes((tm, tk), BF16) + _nbytes((tk, tn), F32) + 2 * _nbytes((tm, tn), F32))
    return pl.pallas_call(
        functools.partial(_down_kernel, nk=nk, k_valid_last=k_valid_last, tk=tk),
        out_shape=jax.ShapeDtypeStruct((t, d), F32),
        grid=(t // tm, d // tn, nk),
        in_specs=[pl.BlockSpec((tm, tk), lambda i, j, k: (i, k)),
                  pl.BlockSpec((tk, tn), lambda i, j, k: (k, j)),
                  pl.BlockSpec((tm, tn), lambda i, j, k: (i, j))],
        out_specs=pl.BlockSpec((tm, tn), lambda i, j, k: (i, j)),
        compiler_params=pltpu.CompilerParams(
            dimension_semantics=("parallel", "parallel", "arbitrary"),
            vmem_limit_bytes=_vmem_limit(blocks, temp_bytes=_nbytes((tk, tn), BF16) + (4 << 20))),
        name="ffn_down",
    )(a, w_down, x)


def _ple_kernel(h_ref, wg_ref, p_ref, wp_ref, x_ref, o_ref):
    gate = _sigmoid(jnp.dot(h_ref[...], wg_ref[...].astype(BF16), preferred_element_type=F32))
    proj = jnp.dot(p_ref[...].astype(BF16), wp_ref[...].astype(BF16), preferred_element_type=F32)
    o_ref[...] = x_ref[...] + gate * proj


def _ple(h, w_gate, p, w_proj, x, *, tm, tn):
    t, d = h.shape
    pd = p.shape[1]
    row_blocks = _nbytes((tm, d), BF16) + _nbytes((tm, pd), F32)
    blocks = _nbytes((d, tn), F32) + _nbytes((pd, tn), F32) + 2 * _nbytes((tm, tn), F32)
    once = dict(pipeline_mode=pl.Buffered(1))
    return pl.pallas_call(
        _ple_kernel,
        out_shape=jax.ShapeDtypeStruct((t, d), F32),
        grid=(t // tm, d // tn),
        in_specs=[pl.BlockSpec((tm, d), lambda i, j: (i, 0), **once),
                  pl.BlockSpec((d, tn), lambda i, j: (0, j)),
                  pl.BlockSpec((tm, pd), lambda i, j: (i, 0), **once),
                  pl.BlockSpec((pd, tn), lambda i, j: (0, j)),
                  pl.BlockSpec((tm, tn), lambda i, j: (i, j))],
        out_specs=pl.BlockSpec((tm, tn), lambda i, j: (i, j)),
        compiler_params=pltpu.CompilerParams(
            dimension_semantics=("parallel", "arbitrary"),
            vmem_limit_bytes=_vmem_limit(blocks, scratch_bytes=row_blocks,
                                         temp_bytes=_nbytes((d, tn), BF16) + 3 * _nbytes((tm, tn), F32))),
        name="ple",
    )(h, w_gate, p, w_proj, x)


def _split3(x):
    hi = x.astype(BF16)
    r = x - hi.astype(F32)
    mid = r.astype(BF16)
    lo = (r - mid.astype(F32)).astype(BF16)
    return hi, mid, lo


def _neg_inclusive_matrix(tk):
    r = lax.broadcasted_iota(jnp.int32, (2 * tk, tk), 0)
    c = lax.broadcasted_iota(jnp.int32, (2 * tk, tk), 1)
    r = jnp.where(r >= tk, r - tk, r)
    return jnp.where(r >= c, -1.0, 0.0).astype(BF16)


def _neg_abs(x):
    bits = lax.bitcast_convert_type(x, jnp.uint32) | jnp.uint32(0x80000000)
    return lax.bitcast_convert_type(bits, F32)


def _sb_scores(q, kblk, *, diagonal):
    z2 = lax.dot_general(q, kblk, (((2,), (2,)), ((0,), (0,))), preferred_element_type=F32)
    heads, tq, tk = z2.shape
    sp2 = jnp.maximum(z2, 0.0) + jnp.log2(1.0 + jnp.exp2(_neg_abs(z2)))
    if diagonal:
        row = lax.broadcasted_iota(jnp.int32, (1, tq, tk), 1)
        col = lax.broadcasted_iota(jnp.int32, (1, tq, tk), 2)
        mask = col < row
        sp2 = jnp.where(mask, sp2, 0.0)
        z2 = jnp.where(mask, z2, -jnp.inf)
    hi = sp2.astype(BF16)
    lo = (sp2 - hi.astype(F32)).astype(BF16)
    parts = jnp.concatenate([hi, lo], axis=-1).reshape(heads * tq, 2 * tk)
    return z2, parts


def _sb_accumulate(z2, parts, vblk, neg_incl, run, acc):
    heads, tq, tk = z2.shape
    incl = jnp.dot(parts, neg_incl, preferred_element_type=F32).reshape(heads, tq, tk)
    a = jnp.exp2((z2 + run) + incl)
    acc = acc + lax.dot_general(a.astype(BF16), vblk, (((2,), (1,)), ((0,), (0,))), preferred_element_type=F32)
    run = run + incl[:, :, 0:1]
    return run, acc


def _head_rmsnorm(x, g):
    return x * lax.rsqrt(jnp.mean(x * x, axis=-1, keepdims=True) + EPS) * g


def _sb_prompt_kernel(q_ref, k_ref, v_ref, g_ref, o_ref, *, tile, heads):
    qi = pl.program_id(2)
    d = SB_HEAD_DIM
    neg_incl = _neg_inclusive_matrix(tile)
    q = jnp.stack([q_ref[:, h * d:(h + 1) * d] for h in range(heads)])

    def key_rows(kb):
        return pl.ds(pl.multiple_of(kb * tile, tile), tile)

    def visit(kb, carry, diagonal):
        kblk = jnp.stack([k_ref[key_rows(kb), h * d:(h + 1) * d] for h in range(heads)])
        vblk = jnp.stack([v_ref[key_rows(kb), h * d:(h + 1) * d] for h in range(heads)])
        z2, parts = _sb_scores(q, kblk, diagonal=diagonal)
        return _sb_accumulate(z2, parts, vblk, neg_incl, *carry)

    carry = (jnp.zeros((heads, tile, 1), F32), jnp.zeros((heads, tile, d), F32))
    carry = visit(qi, carry, True)
    carry = lax.fori_loop(0, qi, lambda t, c: visit(qi - 1 - t, c, False), carry)
    for h in range(heads):
        cols = slice(h * d, (h + 1) * d)
        o_ref[:, cols] = _head_rmsnorm(carry[1][h], g_ref[:, cols]).astype(o_ref.dtype)


def _sb_prompt(q, k, v, g_heads, *, t_all, mix_width, batch, seq, tile, heads_per_step):
    d = SB_HEAD_DIM
    w = heads_per_step * d
    groups = k.shape[1] // w
    nq = seq // tile
    blocks = (2 * _nbytes((tile, w), BF16) + 2 * _nbytes((seq, w), BF16) + _nbytes((8, w), F32))
    return pl.pallas_call(
        functools.partial(_sb_prompt_kernel, tile=tile, heads=heads_per_step),
        out_shape=jax.ShapeDtypeStruct((t_all, mix_width), BF16),
        grid=(batch, groups, nq),
        in_specs=[pl.BlockSpec((tile, w), lambda b, h, i: (b * nq + i, h)),
                  pl.BlockSpec((seq, w), lambda b, h, i: (b, h)),
                  pl.BlockSpec((seq, w), lambda b, h, i: (b, h)),
                  pl.BlockSpec((None, 1, w), lambda b, h, i: (h, 0, 0))],
        out_specs=pl.BlockSpec((tile, w), lambda b, h, i: (b * nq + i, h)),
        compiler_params=pltpu.CompilerParams(
            dimension_semantics=("parallel", "parallel", "arbitrary"),
            vmem_limit_bytes=_vmem_limit(blocks, temp_bytes=16 * heads_per_step * _nbytes((tile, tile), F32))),
        name="sb_prompt",
    )(q, k, v, g_heads.reshape(groups, 1, w))


def _sb_sample_kernel(q_ref, kn_ref, vn_ref, kc_ref, vc_ref, g_ref, mixed_ref, o_ref, run_ref, acc_ref,
                      *, heads, tile):
    del mixed_ref
    kc = pl.program_id(1)
    nkc = pl.num_programs(1)
    d = SB_HEAD_DIM
    n_new = kn_ref.shape[0]
    q = jnp.stack([q_ref[:, h * d:(h + 1) * d] for h in range(heads)])

    @pl.when(kc == 0)
    def _():
        pad = jnp.zeros((heads, V7X_LANES - n_new, d), BF16)
        kblk = jnp.concatenate([jnp.stack([kn_ref[:, h * d:(h + 1) * d] for h in range(heads)]).astype(BF16), pad], axis=1)
        vblk = jnp.concatenate([jnp.stack([vn_ref[:, h * d:(h + 1) * d] for h in range(heads)]).astype(BF16), pad], axis=1)
        z2, parts = _sb_scores(q, kblk, diagonal=True)
        run, acc = _sb_accumulate(z2, parts, vblk, _neg_inclusive_matrix(V7X_LANES),
                                  jnp.zeros((heads, n_new, 1), F32), jnp.zeros((heads, n_new, d), F32))
        run_ref[...] = run
        acc_ref[...] = acc

    kblk = jnp.stack([kc_ref[pl.ds(h, tile, stride=heads), :] for h in range(heads)]).astype(BF16)
    vblk = jnp.stack([vc_ref[pl.ds(h, tile, stride=heads), :] for h in range(heads)]).astype(BF16)
    z2, parts = _sb_scores(q, kblk, diagonal=False)
    run, acc = _sb_accumulate(z2, parts, vblk, _neg_inclusive_matrix(tile), run_ref[...], acc_ref[...])
    run_ref[...] = run
    acc_ref[...] = acc

    @pl.when(kc == nkc - 1)
    def _():
        for h in range(heads):
            cols = slice(h * d, (h + 1) * d)
            o_ref[:, cols] = _head_rmsnorm(acc_ref[h], g_ref[h:h + 1, :]).astype(o_ref.dtype)


def _sb_sample(q, k_new, v_new, k_cache, v_cache, g_heads, mixed, *, layer, row_offset, tile):
    _, batch, past, heads, d = k_cache.shape
    width = heads * d
    n_new = k_new.shape[0] // batch
    roff = row_offset // n_new
    nkc = past // tile
    blocks = (2 * _nbytes((n_new, width), BF16) + 2 * _nbytes((n_new, width), F32)
              + 2 * _nbytes((tile, heads, d), F32) + _nbytes((heads, d), F32))
    state_bytes = _nbytes((heads, n_new, V7X_LANES), F32) + _nbytes((heads, n_new, d), F32)
    k_cache = k_cache.reshape(-1, batch, past * heads, d)
    v_cache = v_cache.reshape(-1, batch, past * heads, d)
    cache_spec = pl.BlockSpec((None, None, tile * heads, d), lambda b, c: (layer, b, nkc - 1 - c, 0))
    return pl.pallas_call(
        functools.partial(_sb_sample_kernel, heads=heads, tile=tile),
        out_shape=jax.ShapeDtypeStruct(mixed.shape, mixed.dtype),
        grid=(batch, nkc),
        in_specs=[pl.BlockSpec((n_new, width), lambda b, c: (b + roff, 0)),
                  pl.BlockSpec((n_new, width), lambda b, c: (b, 0)),
                  pl.BlockSpec((n_new, width), lambda b, c: (b, 0)),
                  cache_spec,
                  cache_spec,
                  pl.BlockSpec((heads, d), lambda b, c: (0, 0)),
                  pl.BlockSpec(memory_space=pl.ANY)],
        out_specs=pl.BlockSpec((n_new, width), lambda b, c: (b + roff, 0)),
        scratch_shapes=[pltpu.VMEM((heads, n_new, 1), F32), pltpu.VMEM((heads, n_new, d), F32)],
        input_output_aliases={6: 0},
        compiler_params=pltpu.CompilerParams(
            dimension_semantics=("parallel", "arbitrary"),
            vmem_limit_bytes=_vmem_limit(blocks, scratch_bytes=state_bytes, temp_bytes=12 << 20)),
        name="sb_sample",
    )(q, k_new, v_new, k_cache, v_cache, g_heads, mixed)


def _log_sigmoid(x):
    return jnp.minimum(x, 0.0) - jnp.log1p(jnp.exp(-jnp.abs(x)))


def _gate_rows(x, chunk):
    pad = (-chunk) % V7X_LANES
    if pad:
        x = jnp.concatenate([x, jnp.zeros((pad, V7X_LANES), x.dtype)], axis=0)
    return x.T[:N_GATES, :chunk]


def _mlstm_kernel(q_ref, k_ref, v_ref, o_ref, g_ref, gain_ref, c0_ref, n0_ref, m0_ref, mixed_ref,
                  out_ref, c_out_ref, n_out_ref, m_out_ref, c_sc, n_sc, m_sc, *, chunk):
    del mixed_ref
    ci = pl.program_id(1)
    nci = pl.num_programs(1)

    @pl.when(ci == 0)
    def _():
        c_sc[...] = c0_ref[...]
        n_sc[...] = n0_ref[...]
        m_sc[...] = m0_ref[...]

    row = lax.broadcasted_iota(jnp.int32, (chunk, chunk), 0)
    col = lax.broadcasted_iota(jnp.int32, (chunk, chunk), 1)
    causal = col <= row
    tri = causal.astype(BF16)

    g = g_ref[...]
    b_cols = sum(jnp.dot(tri, part, preferred_element_type=F32) for part in _split3(_log_sigmoid(g)))
    g_rows = _gate_rows(g, chunk)
    b_rows = _gate_rows(b_cols, chunk)

    for h in range(ML_HEADS):
        qk = slice(h * ML_QK_DIM, (h + 1) * ML_QK_DIM)
        vv = slice(h * ML_V_DIM, (h + 1) * ML_V_DIM)
        q = q_ref[:, qk]
        k = k_ref[:, qk]
        v = v_ref[:, vv]
        c_prev = c_sc[h]
        n_prev = n_sc[h:h + 1, :]
        m_prev = m_sc[h:h + 1, 0:1]

        b_col = b_cols[:, ML_HEADS + h:ML_HEADS + h + 1]
        i_col = g[:, h:h + 1]
        b_row = b_rows[ML_HEADS + h:ML_HEADS + h + 1, :]
        i_row = g_rows[h:h + 1, :]

        d_log = jnp.where(causal, b_col - b_row + i_row, -jnp.inf)
        inter_log = b_col + m_prev
        m_row = jnp.maximum(inter_log, jnp.max(d_log, axis=-1, keepdims=True))
        w_intra = jnp.exp(d_log - m_row)
        w_inter = jnp.exp(inter_log - m_row)

        s = lax.dot_general(q, k, (((1,), (1,)), ((), ())), preferred_element_type=F32) * w_intra
        inter = lax.dot_general(q, c_prev.astype(BF16), (((1,), (1,)), ((), ())), preferred_element_type=F32)
        num = w_inter * inter + jnp.dot(s.astype(BF16), v, preferred_element_type=F32)
        den = w_inter * jnp.sum(q.astype(F32) * n_prev, axis=-1, keepdims=True) + jnp.sum(s, axis=-1, keepdims=True)
        hid = num / jnp.maximum(jnp.abs(den), jnp.exp(-m_row))

        m_new = m_row[chunk - 1:chunk, :]
        b_last = b_col[chunk - 1:chunk, :]
        w_prev = jnp.exp(b_last + m_prev - m_new)
        w_tok = jnp.exp(b_last - b_col + i_col - m_new)
        vw = (v.astype(F32) * w_tok).astype(BF16)
        c_sc[h] = w_prev * c_prev + lax.dot_general(vw, k, (((0,), (0,)), ((), ())), preferred_element_type=F32)
        n_sc[h:h + 1, :] = w_prev * n_prev + jnp.sum(k.astype(F32) * w_tok, axis=0, keepdims=True)
        m_sc[h:h + 1, :] = jnp.broadcast_to(m_new, (1, m_sc.shape[1]))

        normed = _head_rmsnorm(hid, gain_ref[h:h + 1, :])
        out_ref[:, vv] = (normed * _sigmoid(o_ref[:, vv])).astype(out_ref.dtype)

    @pl.when(ci == nci - 1)
    def _():
        c_out_ref[...] = c_sc[...]
        n_out_ref[...] = n_sc[...]
        m_out_ref[...] = m_sc[...]


def _mlstm(q, k, v, o, g, gain, c0, n0, m0, mixed, *, batch, seq, chunk, row_offset, col_offset):
    nci = seq // chunk
    roff = row_offset // chunk
    vw = ML_HEADS * ML_V_DIM
    qw = ML_HEADS * ML_QK_DIM
    coff = col_offset // vw
    state_bytes = (_nbytes((ML_HEADS, ML_V_DIM, ML_QK_DIM), F32) + _nbytes((ML_HEADS, ML_QK_DIM), F32)
                   + _nbytes((ML_HEADS, V7X_LANES), F32))
    blocks = (2 * _nbytes((chunk, qw), BF16) + 2 * _nbytes((chunk, vw), BF16) + _nbytes((chunk, vw), F32)
              + _nbytes((chunk, V7X_LANES), F32) + 2 * state_bytes)
    row_map = lambda b, c: (b * nci + c + roff, 0)
    state_specs = [pl.BlockSpec((None, ML_HEADS, ML_V_DIM, ML_QK_DIM), lambda b, c: (b, 0, 0, 0)),
                   pl.BlockSpec((None, ML_HEADS, ML_QK_DIM), lambda b, c: (b, 0, 0)),
                   pl.BlockSpec((None, ML_HEADS, V7X_LANES), lambda b, c: (b, 0, 0))]
    return pl.pallas_call(
        functools.partial(_mlstm_kernel, chunk=chunk),
        out_shape=[jax.ShapeDtypeStruct(mixed.shape, mixed.dtype),
                   jax.ShapeDtypeStruct(c0.shape, F32),
                   jax.ShapeDtypeStruct(n0.shape, F32),
                   jax.ShapeDtypeStruct(m0.shape, F32)],
        grid=(batch, nci),
        in_specs=[pl.BlockSpec((chunk, qw), row_map),
                  pl.BlockSpec((chunk, qw), row_map),
                  pl.BlockSpec((chunk, vw), row_map),
                  pl.BlockSpec((chunk, vw), row_map),
                  pl.BlockSpec((chunk, V7X_LANES), row_map),
                  pl.BlockSpec((ML_HEADS, ML_V_DIM), lambda b, c: (0, 0)),
                  *state_specs,
                  pl.BlockSpec(memory_space=pl.ANY)],
        out_specs=[pl.BlockSpec((chunk, vw), lambda b, c: (b * nci + c + roff, coff)), *state_specs],
        scratch_shapes=[pltpu.VMEM((ML_HEADS, ML_V_DIM, ML_QK_DIM), F32),
                        pltpu.VMEM((ML_HEADS, ML_QK_DIM), F32),
                        pltpu.VMEM((ML_HEADS, V7X_LANES), F32)],
        input_output_aliases={9: 0},
        compiler_params=pltpu.CompilerParams(
            dimension_semantics=("parallel", "arbitrary"),
            vmem_limit_bytes=_vmem_limit(blocks, scratch_bytes=state_bytes, temp_bytes=12 << 20)),
        name="mlstm",
    )(q, k, v, o, g, gain, c0, n0, m0, mixed)


def _pick_row_tile(t, target):
    best = 16
    for cand in range(16, target + 1, 16):
        if t % cand == 0:
            best = cand
    return best


def kernel(x_prompt, x_sample, cache_sb_k, cache_sb_v, state_ml_c, state_ml_n, state_ml_m, p_prompt, p_sample, ln_ffn1, w_ffn1_gate, w_ffn1_up, w_ffn1_down, ln_mix, w_in, b_if, g_sb_head, g_ml_head, w_out, ln_ffn2, w_ffn2_gate, w_ffn2_up, w_ffn2_down, ln_ple, w_ple_gate, w_ple_proj, ln_final):
    depth = w_in.shape[0]
    batch, seq, d_model = x_prompt.shape
    dec_batch, dec_seq, _ = x_sample.shape
    tp = batch * seq
    ts = dec_batch * dec_seq
    t_all = tp + ts
    sb_heads = g_sb_head.shape[1]
    sb_width = sb_heads * SB_HEAD_DIM
    ml_qk_width = ML_HEADS * ML_QK_DIM
    ml_v_width = ML_HEADS * ML_V_DIM
    mix_width = sb_width + ml_v_width
    d_ff = w_ffn1_gate.shape[2]
    off_q, off_k, off_v = 0, sb_width, 2 * sb_width
    off_mq = 3 * sb_width
    off_mk = off_mq + ml_qk_width
    off_mv = off_mk + ml_qk_width
    off_mo = off_mv + ml_v_width
    off_g = off_mo + ml_v_width

    tm_tall = _pick_row_tile(t_all, 2176)
    tm_prompt = _pick_row_tile(tp, 2176)
    norm_tile = _pick_row_tile(t_all, 384)
    ff_tn = V7X_MXU_DIM
    sb_tile = V7X_MXU_DIM
    ml_chunk = min(seq, V7X_MXU_DIM)

    x = jnp.concatenate([x_prompt.reshape(tp, d_model), x_sample.reshape(ts, d_model)], axis=0)

    def half_ffn(x, ln, w_gate, w_up, w_down):
        h = _rmsnorm(x, ln, BF16, rows=t_all, row_tile=norm_tile)
        a = _matmul_fullk(h, [w_gate, w_up], [0, 0], d_ff, [BF16], _ep_half_swiglu, tm=tm_tall, tn=ff_tn,
                          single_buffer_lhs=True, name="ffn_gate_up")[0]
        return _ffn_down(a, w_down, x, tm=tm_tall, tn=1024, tk=1024)

    outs = {name: [] for name in ("pk", "pv", "pc", "pn", "pm", "sk", "sv", "sc", "sn", "sm")}
    for l in range(depth):
        x = half_ffn(x, ln_ffn1[l], w_ffn1_gate[l], w_ffn1_up[l], w_ffn1_down[l])

        h = _rmsnorm(x, ln_mix[l], BF16, rows=t_all, row_tile=norm_tile)
        w = jnp.swapaxes(w_in[l], 0, 1)
        proj = functools.partial(_matmul_fullk, h, w_transposed=True)
        tall = dict(tm=tm_tall, single_buffer_lhs=True)
        (sb_q,) = proj([w], [off_q], sb_width, [BF16], _ep_sb_q, tn=256, name="proj_sb_q", **tall)
        k_p, v_p, k_pb, v_pb = proj([w, w], [off_k, off_v], sb_width, [F32, F32, BF16, BF16], _ep_kv,
                                    tm=tm_prompt, single_buffer_lhs=True, tn=256, rows=tp, name="proj_sb_kv_prompt")
        k_s, v_s = proj([w, w], [off_k, off_v], sb_width, [F32, F32], _ep_identity,
                        tm=ts, tn=256, rows=ts, row_offset=tp, name="proj_sb_kv_sample")
        ml_q, ml_k = proj([w, w], [off_mq, off_mk], ml_qk_width, [BF16, BF16], _ep_ml_qk,
                          tn=256, name="proj_ml_qk", **tall)
        ml_v, ml_o = proj([w, w], [off_mv, off_mo], ml_v_width, [BF16, F32], _ep_identity,
                          tn=256, name="proj_ml_vo", **tall)
        b_row = jnp.pad(b_if[l], (0, V7X_LANES - N_GATES)).reshape(1, V7X_LANES)
        (gates,) = proj([w], [off_g], V7X_LANES, [F32], _ep_bias, extras=(b_row,),
                        tn=V7X_LANES, name="proj_ml_gates", **tall)

        mixed = _sb_prompt(sb_q, k_pb, v_pb, g_sb_head[l], t_all=t_all, mix_width=mix_width, batch=batch, seq=seq,
                           tile=sb_tile, heads_per_step=8)
        mixed = _sb_sample(sb_q, k_s, v_s, cache_sb_k, cache_sb_v, g_sb_head[l], mixed,
                           layer=l, row_offset=tp, tile=sb_tile)
        c0 = jnp.zeros((batch, ML_HEADS, ML_V_DIM, ML_QK_DIM), F32)
        n0 = jnp.zeros((batch, ML_HEADS, ML_QK_DIM), F32)
        m0 = jnp.zeros((batch, ML_HEADS, V7X_LANES), F32)
        mixed, c_p, n_p, m_p = _mlstm(ml_q, ml_k, ml_v, ml_o, gates, g_ml_head[l], c0, n0, m0, mixed,
                                      batch=batch, seq=seq, chunk=ml_chunk, row_offset=0, col_offset=sb_width)
        m0_s = jnp.broadcast_to(state_ml_m[l][:, :, None], (dec_batch, ML_HEADS, V7X_LANES))
        mixed, c_s, n_s, m_s = _mlstm(ml_q, ml_k, ml_v, ml_o, gates, g_ml_head[l], state_ml_c[l], state_ml_n[l],
                                      m0_s, mixed, batch=dec_batch, seq=dec_seq, chunk=dec_seq,
                                      row_offset=tp, col_offset=sb_width)

        (x,) = _matmul_fullk(mixed, [w_out[l]], [0], d_model, [F32], _ep_residual, extras=(x,),
                             tn=256, name="out_proj", **tall)
        x = half_ffn(x, ln_ffn2[l], w_ffn2_gate[l], w_ffn2_up[l], w_ffn2_down[l])

        h = _rmsnorm(x, ln_ple[l], BF16, rows=t_all, row_tile=norm_tile)
        p = jnp.concatenate([p_prompt[l].reshape(tp, -1), p_sample[l].reshape(ts, -1)], axis=0)
        x = _ple(h, w_ple_gate[l], p, w_ple_proj[l], x, tm=tm_tall, tn=256)

        outs["pk"].append(k_p.reshape(batch, seq, sb_heads, SB_HEAD_DIM))
        outs["pv"].append(v_p.reshape(batch, seq, sb_heads, SB_HEAD_DIM))
        outs["pc"].append(c_p)
        outs["pn"].append(n_p)
        outs["pm"].append(m_p[:, :, 0])
        outs["sk"].append(k_s.reshape(dec_batch, dec_seq, sb_heads, SB_HEAD_DIM))
        outs["sv"].append(v_s.reshape(dec_batch, dec_seq, sb_heads, SB_HEAD_DIM))
        outs["sc"].append(c_s)
        outs["sn"].append(n_s)
        outs["sm"].append(m_s[:, :, 0])

    y_prompt = _rmsnorm(x, ln_final, F32, rows=tp, row_tile=256).reshape(batch, seq, d_model)
    y_sample = _rmsnorm(x, ln_final, F32, rows=ts, row_tile=ts, row_offset=tp).reshape(dec_batch, dec_seq, d_model)

    def stack(name):
        vals = outs[name]
        return vals[0][None] if len(vals) == 1 else jnp.stack(vals)

    return (y_prompt, y_sample, stack("pk"), stack("pv"), stack("pc"), stack("pn"), stack("pm"),
            stack("sk"), stack("sv"), stack("sc"), stack("sn"), stack("sm"))
```

```python
import functools
import math

import jax
import jax.numpy as jnp
from jax import lax
from jax.experimental import pallas as pl
from jax.experimental.pallas import tpu as pltpu

F32 = jnp.float32
BF16 = jnp.bfloat16
EPS = 1e-6
LOG2_E = math.log2(math.e)

V7X_VMEM_BYTES = 64 * 1024 * 1024
V7X_LANES = 128
V7X_MXU_DIM = 256
VMEM_REQUEST_CAP = V7X_VMEM_BYTES - 6 * 1024 * 1024

SB_HEAD_DIM = 128
ML_QK_DIM = 256
ML_V_DIM = 512
ML_HEADS = 4
N_GATES = 2 * ML_HEADS
SB_Q_SCALE = LOG2_E * SB_HEAD_DIM ** -0.5


def _vmem_limit(block_bytes, scratch_bytes=0, temp_bytes=0):
    est = 2 * block_bytes + scratch_bytes + temp_bytes + (2 << 20)
    return int(min(max(est, 16 << 20), VMEM_REQUEST_CAP))


def _nbytes(shape, dtype):
    n = 1
    for s in shape:
        n *= s
    return n * jnp.dtype(dtype).itemsize


def _sigmoid(x):
    return 1.0 / (1.0 + jnp.exp(-x))


def _rmsnorm_kernel(x_ref, g_ref, o_ref):
    x = x_ref[...]
    y = x * lax.rsqrt(jnp.mean(x * x, axis=-1, keepdims=True) + EPS)
    o_ref[...] = (y * g_ref[...]).astype(o_ref.dtype)


def _rmsnorm(x, g, out_dtype, *, rows, row_tile, row_offset=0):
    d = x.shape[1]
    off = row_offset // row_tile
    blocks = _nbytes((row_tile, d), F32) + _nbytes((row_tile, d), out_dtype)
    return pl.pallas_call(
        _rmsnorm_kernel,
        out_shape=jax.ShapeDtypeStruct((rows, d), out_dtype),
        grid=(rows // row_tile,),
        in_specs=[pl.BlockSpec((row_tile, d), lambda i: (i + off, 0)),
                  pl.BlockSpec((1, d), lambda i: (0, 0))],
        out_specs=pl.BlockSpec((row_tile, d), lambda i: (i, 0)),
        compiler_params=pltpu.CompilerParams(
            dimension_semantics=("parallel",),
            vmem_limit_bytes=_vmem_limit(blocks, temp_bytes=2 * _nbytes((row_tile, d), F32))),
        name="rmsnorm",
    )(x, g.reshape(1, d))


def _stack_kernel(xp_ref, xs_ref, g_ref, x_ref, h_ref, *, n_prompt_tiles):
    i = pl.program_id(0)

    def emit(src_ref):
        x_ref[...] = src_ref[...]
        _rmsnorm_kernel(src_ref, g_ref, h_ref)

    pl.when(i < n_prompt_tiles)(lambda: emit(xp_ref))
    pl.when(i >= n_prompt_tiles)(lambda: emit(xs_ref))


def _stack_tokens(x_prompt, x_sample, gain, *, tile):
    tp, d = x_prompt.shape
    ts = x_sample.shape[0]
    n_p, n_s = tp // tile, ts // tile
    t_all = tp + ts
    blocks = 3 * _nbytes((tile, d), F32) + _nbytes((tile, d), BF16)
    row = lambda i: (i, 0)
    return pl.pallas_call(
        functools.partial(_stack_kernel, n_prompt_tiles=n_p),
        out_shape=[jax.ShapeDtypeStruct((t_all, d), F32), jax.ShapeDtypeStruct((t_all, d), BF16)],
        grid=(n_p + n_s,),
        in_specs=[pl.BlockSpec((tile, d), lambda i: (jnp.minimum(i, n_p - 1), 0)),
                  pl.BlockSpec((tile, d), lambda i: (jnp.maximum(i - n_p, 0), 0)),
                  pl.BlockSpec((1, d), lambda i: (0, 0))],
        out_specs=[pl.BlockSpec((tile, d), row), pl.BlockSpec((tile, d), row)],
        compiler_params=pltpu.CompilerParams(
            dimension_semantics=("parallel",),
            vmem_limit_bytes=_vmem_limit(blocks, temp_bytes=2 * _nbytes((tile, d), F32))),
        name="stack_tokens",
    )(x_prompt, x_sample, gain.reshape(1, d))


def _mm_kernel(*refs, n_w, n_extra, epilogue, w_transposed):
    lhs_ref = refs[0]
    w_refs = refs[1:1 + n_w]
    extra_refs = refs[1 + n_w:1 + n_w + n_extra]
    out_refs = refs[1 + n_w + n_extra:]
    lhs = lhs_ref[...]
    w_contract = 1 if w_transposed else 0
    accs = [lax.dot_general(lhs, w[...].astype(BF16), (((1,), (w_contract,)), ((), ())),
                            preferred_element_type=F32) for w in w_refs]
    outs = epilogue(accs, [e[...] for e in extra_refs])
    for o_ref, o in zip(out_refs, outs):
        o_ref[...] = o.astype(o_ref.dtype)


def _matmul_fullk(lhs, ws, col_offsets, n_cols, out_dtypes, epilogue, extras=(), *,
                  tm, tn, rows=None, row_offset=0, w_transposed=False, single_buffer_lhs=False, name):
    k_dim = lhs.shape[1]
    rows = lhs.shape[0] if rows is None else rows
    roff = row_offset // tm
    lhs_mode = dict(pipeline_mode=pl.Buffered(1)) if single_buffer_lhs else {}
    in_specs = [pl.BlockSpec((tm, k_dim), lambda i, j: (i + roff, 0), **lhs_mode)]
    for off in col_offsets:
        if w_transposed:
            in_specs.append(pl.BlockSpec((tn, k_dim), lambda i, j, o=off // tn: (j + o, 0)))
        else:
            in_specs.append(pl.BlockSpec((k_dim, tn), lambda i, j, o=off // tn: (0, j + o)))
    extra_bytes = 0
    for e in extras:
        if e.shape[0] == 1:
            in_specs.append(pl.BlockSpec((1, tn), lambda i, j: (0, j)))
            extra_bytes += _nbytes((8, tn), F32)
        else:
            in_specs.append(pl.BlockSpec((tm, tn), lambda i, j: (i + roff, j)))
            extra_bytes += _nbytes((tm, tn), F32)
    out_specs = [pl.BlockSpec((tm, tn), lambda i, j: (i, j)) for _ in out_dtypes]
    out_shape = [jax.ShapeDtypeStruct((rows, n_cols), dt) for dt in out_dtypes]
    out_bytes = sum(_nbytes((tm, tn), dt) for dt in out_dtypes)
    blocks = _nbytes((tm, k_dim), BF16) + len(ws) * _nbytes((k_dim, tn), F32) + extra_bytes + out_bytes
    temps = (len(ws) * ((3 if w_transposed else 1) * _nbytes((k_dim, tn), BF16) + 4 * _nbytes((tm, tn), F32))
             + len(extras) * _nbytes((tm, tn), F32))
    if single_buffer_lhs:
        temps -= _nbytes((tm, k_dim), BF16)
    return pl.pallas_call(
        functools.partial(_mm_kernel, n_w=len(ws), n_extra=len(extras), epilogue=epilogue,
                          w_transposed=w_transposed),
        out_shape=out_shape,
        grid=(rows // tm, n_cols // tn),
        in_specs=in_specs,
        out_specs=out_specs,
        compiler_params=pltpu.CompilerParams(
            dimension_semantics=("parallel", "arbitrary"),
            vmem_limit_bytes=_vmem_limit(blocks, temp_bytes=temps)),
        name=name,
    )(lhs, *ws, *extras)


def _ep_half_swiglu(accs, extras):
    g, u = accs
    return [0.5 * (g * _sigmoid(g) * u)]


def _ep_identity(accs, extras):
    return list(accs)


def _ep_kv(accs, extras):
    k, v = accs
    return [k, v, k, v]


def _ep_residual(accs, extras):
    return [extras[0] + accs[0]]


def _ep_bias(accs, extras):
    return [accs[0] + extras[0]]


def _ep_sb_q(accs, extras):
    return [accs[0] * SB_Q_SCALE]


def _ep_ml_qk(accs, extras):
    q, k = accs
    return [q * (ML_QK_DIM ** -0.5), k]


def _down_kernel(a_ref, w_ref, x_ref, o_ref, *, nk, k_valid_last, tk):
    k = pl.program_id(2)

    def partial_product(masked):
        a = a_ref[...]
        w = w_ref[...]
        if masked:
            a = jnp.where(lax.broadcasted_iota(jnp.int32, a.shape, 1) < k_valid_last, a, jnp.zeros_like(a))
            w = jnp.where(lax.broadcasted_iota(jnp.int32, w.shape, 0) < k_valid_last, w, jnp.zeros_like(w))
        return jnp.dot(a, w.astype(BF16), preferred_element_type=F32)

    @pl.when(k == 0)
    def _():
        o_ref[...] = x_ref[...] + partial_product(nk == 1 and k_valid_last != tk)

    if nk > 1:
        if k_valid_last == tk:
            @pl.when(k > 0)
            def _():
                o_ref[...] += partial_product(False)
        else:
            @pl.when(jnp.logical_and(k > 0, k < nk - 1))
            def _():
                o_ref[...] += partial_product(False)

            @pl.when(k == nk - 1)
            def _():
                o_ref[...] += partial_product(True)


def _ffn_down(a, w_down, x, *, tm, tn, tk):
    t, f = a.shape
    d = w_down.shape[1]
    nk = pl.cdiv(f, tk)
    k_valid_last = f - (nk - 1) * tk
    blocks = (_nbytes((tm, tk), BF16) + _nbytes((tk, tn), F32) + 2 * _nbytes((tm, tn), F32))
    return pl.pallas_call(
        functools.partial(_down_kernel, nk=nk, k_valid_last=k_valid_last, tk=tk),
        out_shape=jax.ShapeDtypeStruct((t, d), F32),
        grid=(t // tm, d // tn, nk),
        in_specs=[pl.BlockSpec((tm, tk), lambda i, j, k: (i, k)),
                  pl.BlockSpec((tk, tn), lambda i, j, k: (k, j)),
                  pl.BlockSpec((tm, tn), lambda i, j, k: (i, j))],
        out_specs=pl.BlockSpec((tm, tn), lambda i, j, k: (i, j)),
        compiler_params=pltpu.CompilerParams(
            dimension_semantics=("parallel", "parallel", "arbitrary"),
            vmem_limit_bytes=_vmem_limit(blocks, temp_bytes=_nbytes((tk, tn), BF16) + (4 << 20))),
        name="ffn_down",
    )(a, w_down, x)


def _ple_kernel(h_ref, wg_ref, p_ref, wp_ref, x_ref, o_ref):
    gate = _sigmoid(jnp.dot(h_ref[...], wg_ref[...].astype(BF16), preferred_element_type=F32))
    proj = jnp.dot(p_ref[...].astype(BF16), wp_ref[...].astype(BF16), preferred_element_type=F32)
    o_ref[...] = x_ref[...] + gate * proj


def _ple(h, w_gate, p, w_proj, x, *, tm, tn):
    t, d = h.shape
    pd = p.shape[1]
    row_blocks = _nbytes((tm, d), BF16) + _nbytes((tm, pd), F32)
    blocks = _nbytes((d, tn), F32) + _nbytes((pd, tn), F32) + 2 * _nbytes((tm, tn), F32)
    once = dict(pipeline_mode=pl.Buffered(1))
    return pl.pallas_call(
        _ple_kernel,
        out_shape=jax.ShapeDtypeStruct((t, d), F32),
        grid=(t // tm, d // tn),
        in_specs=[pl.BlockSpec((tm, d), lambda i, j: (i, 0), **once),
                  pl.BlockSpec((d, tn), lambda i, j: (0, j)),
                  pl.BlockSpec((tm, pd), lambda i, j: (i, 0), **once),
                  pl.BlockSpec((pd, tn), lambda i, j: (0, j)),
                  pl.BlockSpec((tm, tn), lambda i, j: (i, j))],
        out_specs=pl.BlockSpec((tm, tn), lambda i, j: (i, j)),
        compiler_params=pltpu.CompilerParams(
            dimension_semantics=("parallel", "arbitrary"),
            vmem_limit_bytes=_vmem_limit(blocks, scratch_bytes=row_blocks,
                                         temp_bytes=_nbytes((d, tn), BF16) + 3 * _nbytes((tm, tn), F32))),
        name="ple",
    )(h, w_gate, p, w_proj, x)


def _split3(x):
    hi = x.astype(BF16)
    r = x - hi.astype(F32)
    mid = r.astype(BF16)
    lo = (r - mid.astype(F32)).astype(BF16)
    return hi, mid, lo


def _neg_inclusive_matrix(tk):
    r = lax.broadcasted_iota(jnp.int32, (2 * tk, tk), 0)
    c = lax.broadcasted_iota(jnp.int32, (2 * tk, tk), 1)
    r = jnp.where(r >= tk, r - tk, r)
    return jnp.where(r >= c, -1.0, 0.0).astype(BF16)


def _neg_abs(x):
    bits = lax.bitcast_convert_type(x, jnp.uint32) | jnp.uint32(0x80000000)
    return lax.bitcast_convert_type(bits, F32)


def _sb_scores(q, kblk, *, diagonal):
    z2 = lax.dot_general(q, kblk, (((2,), (2,)), ((0,), (0,))), preferred_element_type=F32)
    heads, tq, tk = z2.shape
    sp2 = jnp.maximum(z2, 0.0) + jnp.log2(1.0 + jnp.exp2(_neg_abs(z2)))
    if diagonal:
        row = lax.broadcasted_iota(jnp.int32, (1, tq, tk), 1)
        col = lax.broadcasted_iota(jnp.int32, (1, tq, tk), 2)
        mask = col < row
        sp2 = jnp.where(mask, sp2, 0.0)
        z2 = jnp.where(mask, z2, -jnp.inf)
    hi = sp2.astype(BF16)
    lo = (sp2 - hi.astype(F32)).astype(BF16)
    parts = jnp.concatenate([hi, lo], axis=-1).reshape(heads * tq, 2 * tk)
    return z2, parts


def _sb_accumulate(z2, parts, vblk, neg_incl, run, acc):
    heads, tq, tk = z2.shape
    incl = jnp.dot(parts, neg_incl, preferred_element_type=F32).reshape(heads, tq, tk)
    a = jnp.exp2((z2 + run) + incl)
    acc = acc + lax.dot_general(a.astype(BF16), vblk, (((2,), (1,)), ((0,), (0,))), preferred_element_type=F32)
    run = run + incl[:, :, 0:1]
    return run, acc


def _head_rmsnorm(x, g):
    return x * lax.rsqrt(jnp.mean(x * x, axis=-1, keepdims=True) + EPS) * g


def _sb_prompt_kernel(q_ref, k_ref, v_ref, g_ref, o_ref, *, tile, heads):
    qi = pl.program_id(2)
    d = SB_HEAD_DIM
    neg_incl = _neg_inclusive_matrix(tile)
    q = jnp.stack([q_ref[:, h * d:(h + 1) * d] for h in range(heads)])

    def key_rows(kb):
        return pl.ds(pl.multiple_of(kb * tile, tile), tile)

    def visit(kb, carry, diagonal):
        kblk = jnp.stack([k_ref[key_rows(kb), h * d:(h + 1) * d] for h in range(heads)])
        vblk = jnp.stack([v_ref[key_rows(kb), h * d:(h + 1) * d] for h in range(heads)])
        z2, parts = _sb_scores(q, kblk, diagonal=diagonal)
        return _sb_accumulate(z2, parts, vblk, neg_incl, *carry)

    def visit_pair(kb, carry):
        return visit(kb - 1, visit(kb, carry, False), False)

    carry = (jnp.zeros((heads, tile, 1), F32), jnp.zeros((heads, tile, d), F32))
    carry = visit(qi, carry, True)
    odd = qi % 2
    carry = lax.cond(odd == 1, lambda c: visit(qi - 1, c, False), lambda c: c, carry)
    first = qi - 1 - odd
    carry = lax.fori_loop(0, qi // 2, lambda t, c: visit_pair(first - 2 * t, c), carry)
    for h in range(heads):
        cols = slice(h * d, (h + 1) * d)
        o_ref[:, cols] = _head_rmsnorm(carry[1][h], g_ref[:, cols]).astype(o_ref.dtype)


def _sb_prompt(q, k, v, g_heads, *, t_all, mix_width, batch, seq, tile, heads_per_step):
    d = SB_HEAD_DIM
    w = heads_per_step * d
    groups = k.shape[1] // w
    nq = seq // tile
    blocks = (2 * _nbytes((tile, w), BF16) + 2 * _nbytes((seq, w), BF16) + _nbytes((8, w), F32))
    return pl.pallas_call(
        functools.partial(_sb_prompt_kernel, tile=tile, heads=heads_per_step),
        out_shape=jax.ShapeDtypeStruct((t_all, mix_width), BF16),
        grid=(batch, groups, nq),
        in_specs=[pl.BlockSpec((tile, w), lambda b, h, i: (b * nq + i, h)),
                  pl.BlockSpec((seq, w), lambda b, h, i: (b, h)),
                  pl.BlockSpec((seq, w), lambda b, h, i: (b, h)),
                  pl.BlockSpec((None, 1, w), lambda b, h, i: (h, 0, 0))],
        out_specs=pl.BlockSpec((tile, w), lambda b, h, i: (b * nq + i, h)),
        compiler_params=pltpu.CompilerParams(
            dimension_semantics=("parallel", "parallel", "arbitrary"),
            vmem_limit_bytes=_vmem_limit(blocks, temp_bytes=16 * heads_per_step * _nbytes((tile, tile), F32))),
        name="sb_prompt",
    )(q, k, v, g_heads.reshape(groups, 1, w))


def _sb_sample_kernel(q_ref, kn_ref, vn_ref, kc_ref, vc_ref, g_ref, mixed_ref, o_ref, run_ref, acc_ref,
                      *, heads, tile):
    del mixed_ref
    kc = pl.program_id(1)
    nkc = pl.num_programs(1)
    d = SB_HEAD_DIM
    n_new = kn_ref.shape[0]
    q = jnp.stack([q_ref[:, h * d:(h + 1) * d] for h in range(heads)])

    @pl.when(kc == 0)
    def _():
        pad = jnp.zeros((heads, V7X_LANES - n_new, d), BF16)
        kblk = jnp.concatenate([jnp.stack([kn_ref[:, h * d:(h + 1) * d] for h in range(heads)]).astype(BF16), pad], axis=1)
        vblk = jnp.concatenate([jnp.stack([vn_ref[:, h * d:(h + 1) * d] for h in range(heads)]).astype(BF16), pad], axis=1)
        z2, parts = _sb_scores(q, kblk, diagonal=True)
        run, acc = _sb_accumulate(z2, parts, vblk, _neg_inclusive_matrix(V7X_LANES),
                                  jnp.zeros((heads, n_new, 1), F32), jnp.zeros((heads, n_new, d), F32))
        run_ref[...] = run
        acc_ref[...] = acc

    kblk = jnp.stack([kc_ref[pl.ds(h, tile, stride=heads), :] for h in range(heads)]).astype(BF16)
    vblk = jnp.stack([vc_ref[pl.ds(h, tile, stride=heads), :] for h in range(heads)]).astype(BF16)
    z2, parts = _sb_scores(q, kblk, diagonal=False)
    run, acc = _sb_accumulate(z2, parts, vblk, _neg_inclusive_matrix(tile), run_ref[...], acc_ref[...])
    run_ref[...] = run
    acc_ref[...] = acc

    @pl.when(kc == nkc - 1)
    def _():
        for h in range(heads):
            cols = slice(h * d, (h + 1) * d)
            o_ref[:, cols] = _head_rmsnorm(acc_ref[h], g_ref[h:h + 1, :]).astype(o_ref.dtype)


def _sb_sample(q, k_new, v_new, k_cache, v_cache, g_heads, mixed, *, layer, row_offset, tile):
    _, batch, past, heads, d = k_cache.shape
    width = heads * d
    n_new = k_new.shape[0] // batch
    roff = row_offset // n_new
    nkc = past // tile
    blocks = (2 * _nbytes((n_new, width), BF16) + 2 * _nbytes((n_new, width), F32)
              + 2 * _nbytes((tile, heads, d), F32) + _nbytes((heads, d), F32))
    state_bytes = _nbytes((heads, n_new, V7X_LANES), F32) + _nbytes((heads, n_new, d), F32)
    k_cache = k_cache.reshape(-1, batch, past * heads, d)
    v_cache = v_cache.reshape(-1, batch, past * heads, d)
    cache_spec = pl.BlockSpec((None, None, tile * heads, d), lambda b, c: (layer, b, nkc - 1 - c, 0))
    return pl.pallas_call(
        functools.partial(_sb_sample_kernel, heads=heads, tile=tile),
        out_shape=jax.ShapeDtypeStruct(mixed.shape, mixed.dtype),
        grid=(batch, nkc),
        in_specs=[pl.BlockSpec((n_new, width), lambda b, c: (b + roff, 0)),
                  pl.BlockSpec((n_new, width), lambda b, c: (b, 0)),
                  pl.BlockSpec((n_new, width), lambda b, c: (b, 0)),
                  cache_spec,
                  cache_spec,
                  pl.BlockSpec((heads, d), lambda b, c: (0, 0)),
                  pl.BlockSpec(memory_space=pl.ANY)],
        out_specs=pl.BlockSpec((n_new, width), lambda b, c: (b + roff, 0)),
        scratch_shapes=[pltpu.VMEM((heads, n_new, 1), F32), pltpu.VMEM((heads, n_new, d), F32)],
        input_output_aliases={6: 0},
        compiler_params=pltpu.CompilerParams(
            dimension_semantics=("parallel", "arbitrary"),
            vmem_limit_bytes=_vmem_limit(blocks, scratch_bytes=state_bytes, temp_bytes=12 << 20)),
        name="sb_sample",
    )(q, k_new, v_new, k_cache, v_cache, g_heads, mixed)


def _log_sigmoid(x):
    return jnp.minimum(x, 0.0) - jnp.log1p(jnp.exp(-jnp.abs(x)))


def _gate_rows(x, chunk):
    pad = (-chunk) % V7X_LANES
    if pad:
        x = jnp.concatenate([x, jnp.zeros((pad, V7X_LANES), x.dtype)], axis=0)
    return x.T[:N_GATES, :chunk]


def _mlstm_kernel(q_ref, k_ref, v_ref, o_ref, g_ref, gain_ref, c0_ref, n0_ref, m0_ref, mixed_ref,
                  out_ref, c_out_ref, n_out_ref, m_out_ref, c_sc, n_sc, m_sc, *, chunk):
    del mixed_ref
    ci = pl.program_id(1)
    nci = pl.num_programs(1)

    @pl.when(ci == 0)
    def _():
        c_sc[...] = c0_ref[...]
        n_sc[...] = n0_ref[...]
        m_sc[...] = m0_ref[...]

    row = lax.broadcasted_iota(jnp.int32, (chunk, chunk), 0)
    col = lax.broadcasted_iota(jnp.int32, (chunk, chunk), 1)
    causal = col <= row
    tri = causal.astype(BF16)

    g = g_ref[...]
    b_cols = sum(jnp.dot(tri, part, preferred_element_type=F32) for part in _split3(_log_sigmoid(g)))
    g_rows = _gate_rows(g, chunk)
    b_rows = _gate_rows(b_cols, chunk)

    for h in range(ML_HEADS):
        qk = slice(h * ML_QK_DIM, (h + 1) * ML_QK_DIM)
        vv = slice(h * ML_V_DIM, (h + 1) * ML_V_DIM)
        q = q_ref[:, qk]
        k = k_ref[:, qk]
        v = v_ref[:, vv]
        c_prev = c_sc[h]
        n_prev = n_sc[h:h + 1, :]
        m_prev = m_sc[h:h + 1, 0:1]

        b_col = b_cols[:, ML_HEADS + h:ML_HEADS + h + 1]
        i_col = g[:, h:h + 1]
        b_row = b_rows[ML_HEADS + h:ML_HEADS + h + 1, :]
        i_row = g_rows[h:h + 1, :]

        d_log = jnp.where(causal, b_col - b_row + i_row, -jnp.inf)
        inter_log = b_col + m_prev
        m_row = jnp.maximum(inter_log, jnp.max(d_log, axis=-1, keepdims=True))
        w_intra = jnp.exp(d_log - m_row)
        w_inter = jnp.exp(inter_log - m_row)

        s = lax.dot_general(q, k, (((1,), (1,)), ((), ())), preferred_element_type=F32) * w_intra
        inter = lax.dot_general(q, c_prev.astype(BF16), (((1,), (1,)), ((), ())), preferred_element_type=F32)
        num = w_inter * inter + jnp.dot(s.astype(BF16), v, preferred_element_type=F32)
        den = w_inter * jnp.sum(q.astype(F32) * n_prev, axis=-1, keepdims=True) + jnp.sum(s, axis=-1, keepdims=True)
        hid = num / jnp.maximum(jnp.abs(den), jnp.exp(-m_row))

        m_new = m_row[chunk - 1:chunk, :]
        b_last = b_col[chunk - 1:chunk, :]
        w_prev = jnp.exp(b_last + m_prev - m_new)
        w_tok = jnp.exp(b_last - b_col + i_col - m_new)
        vw = (v.astype(F32) * w_tok).astype(BF16)
        c_sc[h] = w_prev * c_prev + lax.dot_general(vw, k, (((0,), (0,)), ((), ())), preferred_element_type=F32)
        n_sc[h:h + 1, :] = w_prev * n_prev + jnp.sum(k.astype(F32) * w_tok, axis=0, keepdims=True)
        m_sc[h:h + 1, :] = jnp.broadcast_to(m_new, (1, m_sc.shape[1]))

        normed = _head_rmsnorm(hid, gain_ref[h:h + 1, :])
        out_ref[:, vv] = (normed * _sigmoid(o_ref[:, vv])).astype(out_ref.dtype)

    @pl.when(ci == nci - 1)
    def _():
        c_out_ref[...] = c_sc[...]
        n_out_ref[...] = n_sc[...]
        m_out_ref[...] = m_sc[...]


def _mlstm(q, k, v, o, g, gain, c0, n0, m0, mixed, *, batch, seq, chunk, row_offset, col_offset):
    nci = seq // chunk
    roff = row_offset // chunk
    vw = ML_HEADS * ML_V_DIM
    qw = ML_HEADS * ML_QK_DIM
    coff = col_offset // vw
    state_bytes = (_nbytes((ML_HEADS, ML_V_DIM, ML_QK_DIM), F32) + _nbytes((ML_HEADS, ML_QK_DIM), F32)
                   + _nbytes((ML_HEADS, V7X_LANES), F32))
    blocks = (2 * _nbytes((chunk, qw), BF16) + 2 * _nbytes((chunk, vw), BF16) + _nbytes((chunk, vw), F32)
              + _nbytes((chunk, V7X_LANES), F32) + 2 * state_bytes)
    row_map = lambda b, c: (b * nci + c + roff, 0)
    state_specs = [pl.BlockSpec((None, ML_HEADS, ML_V_DIM, ML_QK_DIM), lambda b, c: (b, 0, 0, 0)),
                   pl.BlockSpec((None, ML_HEADS, ML_QK_DIM), lambda b, c: (b, 0, 0)),
                   pl.BlockSpec((None, ML_HEADS, V7X_LANES), lambda b, c: (b, 0, 0))]
    return pl.pallas_call(
        functools.partial(_mlstm_kernel, chunk=chunk),
        out_shape=[jax.ShapeDtypeStruct(mixed.shape, mixed.dtype),
                   jax.ShapeDtypeStruct(c0.shape, F32),
                   jax.ShapeDtypeStruct(n0.shape, F32),
                   jax.ShapeDtypeStruct(m0.shape, F32)],
        grid=(batch, nci),
        in_specs=[pl.BlockSpec((chunk, qw), row_map),
                  pl.BlockSpec((chunk, qw), row_map),
                  pl.BlockSpec((chunk, vw), row_map),
                  pl.BlockSpec((chunk, vw), row_map),
                  pl.BlockSpec((chunk, V7X_LANES), row_map),
                  pl.BlockSpec((ML_HEADS, ML_V_DIM), lambda b, c: (0, 0)),
                  *state_specs,
                  pl.BlockSpec(memory_space=pl.ANY)],
        out_specs=[pl.BlockSpec((chunk, vw), lambda b, c: (b * nci + c + roff, coff)), *state_specs],
        scratch_shapes=[pltpu.VMEM((ML_HEADS, ML_V_DIM, ML_QK_DIM), F32),
                        pltpu.VMEM((ML_HEADS, ML_QK_DIM), F32),
                        pltpu.VMEM((ML_HEADS, V7X_LANES), F32)],
        input_output_aliases={9: 0},
        compiler_params=pltpu.CompilerParams(
            dimension_semantics=("parallel", "arbitrary"),
            vmem_limit_bytes=_vmem_limit(blocks, scratch_bytes=state_bytes, temp_bytes=12 << 20)),
        name="mlstm",
    )(q, k, v, o, g, gain, c0, n0, m0, mixed)


def _pick_row_tile(t, target):
    best = 16
    for cand in range(16, target + 1, 16):
        if t % cand == 0:
            best = cand
    return best


def kernel(x_prompt, x_sample, cache_sb_k, cache_sb_v, state_ml_c, state_ml_n, state_ml_m, p_prompt, p_sample, ln_ffn1, w_ffn1_gate, w_ffn1_up, w_ffn1_down, ln_mix, w_in, b_if, g_sb_head, g_ml_head, w_out, ln_ffn2, w_ffn2_gate, w_ffn2_up, w_ffn2_down, ln_ple, w_ple_gate, w_ple_proj, ln_final):
    depth = w_in.shape[0]
    batch, seq, d_model = x_prompt.shape
    dec_batch, dec_seq, _ = x_sample.shape
    tp = batch * seq
    ts = dec_batch * dec_seq
    t_all = tp + ts
    sb_heads = g_sb_head.shape[1]
    sb_width = sb_heads * SB_HEAD_DIM
    ml_qk_width = ML_HEADS * ML_QK_DIM
    ml_v_width = ML_HEADS * ML_V_DIM
    mix_width = sb_width + ml_v_width
    d_ff = w_ffn1_gate.shape[2]
    off_q, off_k, off_v = 0, sb_width, 2 * sb_width
    off_mq = 3 * sb_width
    off_mk = off_mq + ml_qk_width
    off_mv = off_mk + ml_qk_width
    off_mo = off_mv + ml_v_width
    off_g = off_mo + ml_v_width

    tm_tall = _pick_row_tile(t_all, 2176)
    tm_prompt = _pick_row_tile(tp, 2176)
    norm_tile = _pick_row_tile(t_all, 384)
    ff_tn = V7X_MXU_DIM
    sb_tile = V7X_MXU_DIM
    ml_chunk = min(seq, V7X_MXU_DIM)
    tall = dict(tm=tm_tall, single_buffer_lhs=True)

    x, h = _stack_tokens(x_prompt.reshape(tp, d_model), x_sample.reshape(ts, d_model), ln_ffn1[0],
                         tile=math.gcd(tp, ts))

    def half_ffn(x, h, w_gate, w_up, w_down):
        a = _matmul_fullk(h, [w_gate, w_up], [0, 0], d_ff, [BF16], _ep_half_swiglu, tn=ff_tn,
                          name="ffn_gate_up", **tall)[0]
        return _ffn_down(a, w_down, x, tm=tm_tall, tn=1024, tk=1024)

    outs = {name: [] for name in ("pk", "pv", "pc", "pn", "pm", "sk", "sv", "sc", "sn", "sm")}
    for l in range(depth):
        if l > 0:
            h = _rmsnorm(x, ln_ffn1[l], BF16, rows=t_all, row_tile=norm_tile)
        x = half_ffn(x, h, w_ffn1_gate[l], w_ffn1_up[l], w_ffn1_down[l])

        h = _rmsnorm(x, ln_mix[l], BF16, rows=t_all, row_tile=norm_tile)
        w = jnp.swapaxes(w_in[l], 0, 1)
        proj = functools.partial(_matmul_fullk, h, w_transposed=True)
        (sb_q,) = proj([w], [off_q], sb_width, [BF16], _ep_sb_q, tn=256, name="proj_sb_q", **tall)
        k_p, v_p, k_pb, v_pb = proj([w, w], [off_k, off_v], sb_width, [F32, F32, BF16, BF16], _ep_kv,
                                    tm=tm_prompt, single_buffer_lhs=True, tn=256, rows=tp, name="proj_sb_kv_prompt")
        k_s, v_s = proj([w, w], [off_k, off_v], sb_width, [F32, F32], _ep_identity,
                        tm=ts, tn=256, rows=ts, row_offset=tp, name="proj_sb_kv_sample")
        ml_q, ml_k = proj([w, w], [off_mq, off_mk], ml_qk_width, [BF16, BF16], _ep_ml_qk,
                          tn=256, name="proj_ml_qk", **tall)
        ml_v, ml_o = proj([w, w], [off_mv, off_mo], ml_v_width, [BF16, F32], _ep_identity,
                          tn=256, name="proj_ml_vo", **tall)
        b_row = jnp.pad(b_if[l], (0, V7X_LANES - N_GATES)).reshape(1, V7X_LANES)
        (gates,) = proj([w], [off_g], V7X_LANES, [F32], _ep_bias, extras=(b_row,),
                        tn=V7X_LANES, name="proj_ml_gates", **tall)

        mixed = _sb_prompt(sb_q, k_pb, v_pb, g_sb_head[l], t_all=t_all, mix_width=mix_width, batch=batch, seq=seq,
                           tile=sb_tile, heads_per_step=8)
        mixed = _sb_sample(sb_q, k_s, v_s, cache_sb_k, cache_sb_v, g_sb_head[l], mixed,
                           layer=l, row_offset=tp, tile=sb_tile)
        c0 = jnp.zeros((batch, ML_HEADS, ML_V_DIM, ML_QK_DIM), F32)
        n0 = jnp.zeros((batch, ML_HEADS, ML_QK_DIM), F32)
        m0 = jnp.zeros((batch, ML_HEADS, V7X_LANES), F32)
        mixed, c_p, n_p, m_p = _mlstm(ml_q, ml_k, ml_v, ml_o, gates, g_ml_head[l], c0, n0, m0, mixed,
                                      batch=batch, seq=seq, chunk=ml_chunk, row_offset=0, col_offset=sb_width)
        m0_s = jnp.broadcast_to(state_ml_m[l][:, :, None], (dec_batch, ML_HEADS, V7X_LANES))
        mixed, c_s, n_s, m_s = _mlstm(ml_q, ml_k, ml_v, ml_o, gates, g_ml_head[l], state_ml_c[l], state_ml_n[l],
                                      m0_s, mixed, batch=dec_batch, seq=dec_seq, chunk=dec_seq,
                                      row_offset=tp, col_offset=sb_width)

        (x,) = _matmul_fullk(mixed, [w_out[l]], [0], d_model, [F32], _ep_residual, extras=(x,),
                             tn=256, name="out_proj", **tall)
        h = _rmsnorm(x, ln_ffn2[l], BF16, rows=t_all, row_tile=norm_tile)
        x = half_ffn(x, h, w_ffn2_gate[l], w_ffn2_up[l], w_ffn2_down[l])

        h = _rmsnorm(x, ln_ple[l], BF16, rows=t_all, row_tile=norm_tile)
        p = jnp.concatenate([p_prompt[l].reshape(tp, -1), p_sample[l].reshape(ts, -1)], axis=0)
        x = _ple(h, w_ple_gate[l], p, w_ple_proj[l], x, tm=tm_tall, tn=256)

        outs["pk"].append(k_p.reshape(batch, seq, sb_heads, SB_HEAD_DIM))
        outs["pv"].append(v_p.reshape(batch, seq, sb_heads, SB_HEAD_DIM))
        outs["pc"].append(c_p)
        outs["pn"].append(n_p)
        outs["pm"].append(m_p[:, :, 0])
        outs["sk"].append(k_s.reshape(dec_batch, dec_seq, sb_heads, SB_HEAD_DIM))
        outs["sv"].append(v_s.reshape(dec_batch, dec_seq, sb_heads, SB_HEAD_DIM))
        outs["sc"].append(c_s)
        outs["sn"].append(n_s)
        outs["sm"].append(m_s[:, :, 0])

    y_prompt = _rmsnorm(x, ln_final, F32, rows=tp, row_tile=256).reshape(batch, seq, d_model)
    y_sample = _rmsnorm(x, ln_final, F32, rows=ts, row_tile=ts, row_offset=tp).reshape(dec_batch, dec_seq, d_model)

    def stack(name):
        vals = outs[name]
        return vals[0][None] if len(vals) == 1 else jnp.stack(vals)

    return (y_prompt, y_sample, stack("pk"), stack("pv"), stack("pc"), stack("pn"), stack("pm"),
            stack("sk"), stack("sv"), stack("sc"), stack("sn"), stack("sm"))
```

```python
import functools
import math

import jax
import jax.numpy as jnp
from jax import lax
from jax.experimental import pallas as pl
from jax.experimental.pallas import tpu as pltpu

F32 = jnp.float32
BF16 = jnp.bfloat16
EPS = 1e-6
LOG2_E = math.log2(math.e)

V7X_VMEM_BYTES = 64 * 1024 * 1024
V7X_LANES = 128
V7X_MXU_DIM = 256
VMEM_REQUEST_CAP = V7X_VMEM_BYTES - 6 * 1024 * 1024

SB_HEAD_DIM = 128
ML_QK_DIM = 256
ML_V_DIM = 512
ML_HEADS = 4
N_GATES = 2 * ML_HEADS
SB_Q_SCALE = LOG2_E * SB_HEAD_DIM ** -0.5
SB_TILES_PER_TRIP = 4


def _vmem_limit(block_bytes, scratch_bytes=0, temp_bytes=0):
    est = 2 * block_bytes + scratch_bytes + temp_bytes + (2 << 20)
    return int(min(max(est, 16 << 20), VMEM_REQUEST_CAP))


def _nbytes(shape, dtype):
    n = 1
    for s in shape:
        n *= s
    return n * jnp.dtype(dtype).itemsize


def _sigmoid(x):
    return 1.0 / (1.0 + jnp.exp(-x))


def _rmsnorm_kernel(x_ref, g_ref, o_ref):
    x = x_ref[...]
    y = x * lax.rsqrt(jnp.mean(x * x, axis=-1, keepdims=True) + EPS)
    o_ref[...] = (y * g_ref[...]).astype(o_ref.dtype)


def _rmsnorm(x, g, out_dtype, *, rows, row_tile, row_offset=0):
    d = x.shape[1]
    off = row_offset // row_tile
    blocks = _nbytes((row_tile, d), F32) + _nbytes((row_tile, d), out_dtype)
    return pl.pallas_call(
        _rmsnorm_kernel,
        out_shape=jax.ShapeDtypeStruct((rows, d), out_dtype),
        grid=(rows // row_tile,),
        in_specs=[pl.BlockSpec((row_tile, d), lambda i: (i + off, 0)),
                  pl.BlockSpec((1, d), lambda i: (0, 0))],
        out_specs=pl.BlockSpec((row_tile, d), lambda i: (i, 0)),
        compiler_params=pltpu.CompilerParams(
            dimension_semantics=("parallel",),
            vmem_limit_bytes=_vmem_limit(blocks, temp_bytes=2 * _nbytes((row_tile, d), F32))),
        name="rmsnorm",
    )(x, g.reshape(1, d))


def _stack_kernel(xp_ref, xs_ref, g_ref, x_ref, h_ref, *, n_prompt_tiles):
    i = pl.program_id(0)

    def emit(src_ref):
        x_ref[...] = src_ref[...]
        _rmsnorm_kernel(src_ref, g_ref, h_ref)

    pl.when(i < n_prompt_tiles)(lambda: emit(xp_ref))
    pl.when(i >= n_prompt_tiles)(lambda: emit(xs_ref))


def _stack_tokens(x_prompt, x_sample, gain, *, tile):
    tp, d = x_prompt.shape
    ts = x_sample.shape[0]
    n_p, n_s = tp // tile, ts // tile
    t_all = tp + ts
    blocks = 3 * _nbytes((tile, d), F32) + _nbytes((tile, d), BF16)
    row = lambda i: (i, 0)
    return pl.pallas_call(
        functools.partial(_stack_kernel, n_prompt_tiles=n_p),
        out_shape=[jax.ShapeDtypeStruct((t_all, d), F32), jax.ShapeDtypeStruct((t_all, d), BF16)],
        grid=(n_p + n_s,),
        in_specs=[pl.BlockSpec((tile, d), lambda i: (jnp.minimum(i, n_p - 1), 0)),
                  pl.BlockSpec((tile, d), lambda i: (jnp.maximum(i - n_p, 0), 0)),
                  pl.BlockSpec((1, d), lambda i: (0, 0))],
        out_specs=[pl.BlockSpec((tile, d), row), pl.BlockSpec((tile, d), row)],
        compiler_params=pltpu.CompilerParams(
            dimension_semantics=("parallel",),
            vmem_limit_bytes=_vmem_limit(blocks, temp_bytes=2 * _nbytes((tile, d), F32))),
        name="stack_tokens",
    )(x_prompt, x_sample, gain.reshape(1, d))


def _mm_kernel(*refs, n_w, n_extra, epilogue, w_transposed):
    lhs_ref = refs[0]
    w_refs = refs[1:1 + n_w]
    extra_refs = refs[1 + n_w:1 + n_w + n_extra]
    out_refs = refs[1 + n_w + n_extra:]
    lhs = lhs_ref[...]
    w_contract = 1 if w_transposed else 0
    accs = [lax.dot_general(lhs, w[...].astype(BF16), (((1,), (w_contract,)), ((), ())),
                            preferred_element_type=F32) for w in w_refs]
    outs = epilogue(accs, [e[...] for e in extra_refs])
    for o_ref, o in zip(out_refs, outs):
        o_ref[...] = o.astype(o_ref.dtype)


def _matmul_fullk(lhs, ws, col_offsets, n_cols, out_dtypes, epilogue, extras=(), *,
                  tm, tn, rows=None, row_offset=0, w_transposed=False, single_buffer_lhs=False, name):
    k_dim = lhs.shape[1]
    rows = lhs.shape[0] if rows is None else rows
    roff = row_offset // tm
    lhs_mode = dict(pipeline_mode=pl.Buffered(1)) if single_buffer_lhs else {}
    in_specs = [pl.BlockSpec((tm, k_dim), lambda i, j: (i + roff, 0), **lhs_mode)]
    for off in col_offsets:
        if w_transposed:
            in_specs.append(pl.BlockSpec((tn, k_dim), lambda i, j, o=off // tn: (j + o, 0)))
        else:
            in_specs.append(pl.BlockSpec((k_dim, tn), lambda i, j, o=off // tn: (0, j + o)))
    extra_bytes = 0
    for e in extras:
        if e.shape[0] == 1:
            in_specs.append(pl.BlockSpec((1, tn), lambda i, j: (0, j)))
            extra_bytes += _nbytes((8, tn), F32)
        else:
            in_specs.append(pl.BlockSpec((tm, tn), lambda i, j: (i + roff, j)))
            extra_bytes += _nbytes((tm, tn), F32)
    out_specs = [pl.BlockSpec((tm, tn), lambda i, j: (i, j)) for _ in out_dtypes]
    out_shape = [jax.ShapeDtypeStruct((rows, n_cols), dt) for dt in out_dtypes]
    out_bytes = sum(_nbytes((tm, tn), dt) for dt in out_dtypes)
    blocks = _nbytes((tm, k_dim), BF16) + len(ws) * _nbytes((k_dim, tn), F32) + extra_bytes + out_bytes
    temps = (len(ws) * ((3 if w_transposed else 1) * _nbytes((k_dim, tn), BF16) + 4 * _nbytes((tm, tn), F32))
             + len(extras) * _nbytes((tm, tn), F32))
    if single_buffer_lhs:
        temps -= _nbytes((tm, k_dim), BF16)
    return pl.pallas_call(
        functools.partial(_mm_kernel, n_w=len(ws), n_extra=len(extras), epilogue=epilogue,
                          w_transposed=w_transposed),
        out_shape=out_shape,
        grid=(rows // tm, n_cols // tn),
        in_specs=in_specs,
        out_specs=out_specs,
        compiler_params=pltpu.CompilerParams(
            dimension_semantics=("parallel", "arbitrary"),
            vmem_limit_bytes=_vmem_limit(blocks, temp_bytes=temps)),
        name=name,
    )(lhs, *ws, *extras)


def _ep_half_swiglu(accs, extras):
    g, u = accs
    return [0.5 * (g * _sigmoid(g) * u)]


def _ep_identity(accs, extras):
    return list(accs)


def _ep_kv(accs, extras):
    k, v = accs
    return [k, v, k, v]


def _ep_residual(accs, extras):
    return [extras[0] + accs[0]]


def _ep_bias(accs, extras):
    return [accs[0] + extras[0]]


def _ep_sb_q(accs, extras):
    return [accs[0] * SB_Q_SCALE]


def _ep_ml_qk(accs, extras):
    q, k = accs
    return [q * (ML_QK_DIM ** -0.5), k]


def _down_kernel(a_ref, w_ref, x_ref, o_ref, *, nk, k_valid_last, tk):
    k = pl.program_id(2)

    def partial_product(masked):
        a = a_ref[...]
        w = w_ref[...]
        if masked:
            a = jnp.where(lax.broadcasted_iota(jnp.int32, a.shape, 1) < k_valid_last, a, jnp.zeros_like(a))
            w = jnp.where(lax.broadcasted_iota(jnp.int32, w.shape, 0) < k_valid_last, w, jnp.zeros_like(w))
        return jnp.dot(a, w.astype(BF16), preferred_element_type=F32)

    @pl.when(k == 0)
    def _():
        o_ref[...] = x_ref[...] + partial_product(nk == 1 and k_valid_last != tk)

    if nk > 1:
        if k_valid_last == tk:
            @pl.when(k > 0)
            def _():
                o_ref[...] += partial_product(False)
        else:
            @pl.when(jnp.logical_and(k > 0, k < nk - 1))
            def _():
                o_ref[...] += partial_product(False)

            @pl.when(k == nk - 1)
            def _():
                o_ref[...] += partial_product(True)


def _ffn_down(a, w_down, x, *, tm, tn, tk):
    t, f = a.shape
    d = w_down.shape[1]
    nk = pl.cdiv(f, tk)
    k_valid_last = f - (nk - 1) * tk
    blocks = (_nbytes((tm, tk), BF16) + _nbytes((tk, tn), F32) + 2 * _nbytes((tm, tn), F32))
    return pl.pallas_call(
        functools.partial(_down_kernel, nk=nk, k_valid_last=k_valid_last, tk=tk),
        out_shape=jax.ShapeDtypeStruct((t, d), F32),
        grid=(t // tm, d // tn, nk),
        in_specs=[pl.BlockSpec((tm, tk), lambda i, j, k: (i, k)),
                  pl.BlockSpec((tk, tn), lambda i, j, k: (k, j)),
                  pl.BlockSpec((tm, tn), lambda i, j, k: (i, j))],
        out_specs=pl.BlockSpec((tm, tn), lambda i, j, k: (i, j)),
        compiler_params=pltpu.CompilerParams(
            dimension_semantics=("parallel", "parallel", "arbitrary"),
            vmem_limit_bytes=_vmem_limit(blocks, temp_bytes=_nbytes((tk, tn), BF16) + (4 << 20))),
        name="ffn_down",
    )(a, w_down, x)


def _ple_kernel(h_ref, wg_ref, p_ref, wp_ref, x_ref, o_ref):
    gate = _sigmoid(jnp.dot(h_ref[...], wg_ref[...].astype(BF16), preferred_element_type=F32))
    proj = jnp.dot(p_ref[...].astype(BF16), wp_ref[...].astype(BF16), preferred_element_type=F32)
    o_ref[...] = x_ref[...] + gate * proj


def _ple(h, w_gate, p, w_proj, x, *, tm, tn):
    t, d = h.shape
    pd = p.shape[1]
    row_blocks = _nbytes((tm, d), BF16) + _nbytes((tm, pd), F32)
    blocks = _nbytes((d, tn), F32) + _nbytes((pd, tn), F32) + 2 * _nbytes((tm, tn), F32)
    once = dict(pipeline_mode=pl.Buffered(1))
    return pl.pallas_call(
        _ple_kernel,
        out_shape=jax.ShapeDtypeStruct((t, d), F32),
        grid=(t // tm, d // tn),
        in_specs=[pl.BlockSpec((tm, d), lambda i, j: (i, 0), **once),
                  pl.BlockSpec((d, tn), lambda i, j: (0, j)),
                  pl.BlockSpec((tm, pd), lambda i, j: (i, 0), **once),
                  pl.BlockSpec((pd, tn), lambda i, j: (0, j)),
                  pl.BlockSpec((tm, tn), lambda i, j: (i, j))],
        out_specs=pl.BlockSpec((tm, tn), lambda i, j: (i, j)),
        compiler_params=pltpu.CompilerParams(
            dimension_semantics=("parallel", "arbitrary"),
            vmem_limit_bytes=_vmem_limit(blocks, scratch_bytes=row_blocks,
                                         temp_bytes=_nbytes((d, tn), BF16) + 3 * _nbytes((tm, tn), F32))),
        name="ple",
    )(h, w_gate, p, w_proj, x)


def _split3(x):
    hi = x.astype(BF16)
    r = x - hi.astype(F32)
    mid = r.astype(BF16)
    lo = (r - mid.astype(F32)).astype(BF16)
    return hi, mid, lo


def _neg_inclusive_matrix(tk):
    r = lax.broadcasted_iota(jnp.int32, (2 * tk, tk), 0)
    c = lax.broadcasted_iota(jnp.int32, (2 * tk, tk), 1)
    r = jnp.where(r >= tk, r - tk, r)
    return jnp.where(r >= c, -1.0, 0.0).astype(BF16)


def _neg_abs(x):
    bits = lax.bitcast_convert_type(x, jnp.uint32) | jnp.uint32(0x80000000)
    return lax.bitcast_convert_type(bits, F32)


def _sb_scores(q, kblk, *, diagonal):
    z2 = lax.dot_general(q, kblk, (((2,), (2,)), ((0,), (0,))), preferred_element_type=F32)
    heads, tq, tk = z2.shape
    sp2 = jnp.maximum(z2, 0.0) + jnp.log2(1.0 + jnp.exp2(_neg_abs(z2)))
    if diagonal:
        row = lax.broadcasted_iota(jnp.int32, (1, tq, tk), 1)
        col = lax.broadcasted_iota(jnp.int32, (1, tq, tk), 2)
        mask = col < row
        sp2 = jnp.where(mask, sp2, 0.0)
        z2 = jnp.where(mask, z2, -jnp.inf)
    hi = sp2.astype(BF16)
    lo = (sp2 - hi.astype(F32)).astype(BF16)
    parts = jnp.concatenate([hi, lo], axis=-1).reshape(heads * tq, 2 * tk)
    return z2, parts


def _sb_accumulate(z2, parts, vblk, neg_incl, run, acc):
    heads, tq, tk = z2.shape
    incl = jnp.dot(parts, neg_incl, preferred_element_type=F32).reshape(heads, tq, tk)
    a = jnp.exp2((z2 + run) + incl)
    acc = acc + lax.dot_general(a.astype(BF16), vblk, (((2,), (1,)), ((0,), (0,))), preferred_element_type=F32)
    run = run + incl[:, :, 0:1]
    return run, acc


def _head_rmsnorm(x, g):
    return x * lax.rsqrt(jnp.mean(x * x, axis=-1, keepdims=True) + EPS) * g


def _sb_prompt_kernel(q_ref, k_ref, v_ref, g_ref, o_ref, *, tile, heads, tiles_per_trip):
    qi = pl.program_id(2)
    d = SB_HEAD_DIM
    neg_incl = _neg_inclusive_matrix(tile)
    q = jnp.stack([q_ref[:, h * d:(h + 1) * d] for h in range(heads)])

    def key_rows(kb):
        return pl.ds(pl.multiple_of(kb * tile, tile), tile)

    def visit(kb, carry, diagonal):
        kblk = jnp.stack([k_ref[key_rows(kb), h * d:(h + 1) * d] for h in range(heads)])
        vblk = jnp.stack([v_ref[key_rows(kb), h * d:(h + 1) * d] for h in range(heads)])
        z2, parts = _sb_scores(q, kblk, diagonal=diagonal)
        return _sb_accumulate(z2, parts, vblk, neg_incl, *carry)

    def visit_group(kb, carry):
        for g in range(tiles_per_trip):
            carry = visit(kb - g, carry, False)
        return carry

    carry = (jnp.zeros((heads, tile, 1), F32), jnp.zeros((heads, tile, d), F32))
    carry = visit(qi, carry, True)
    singles = qi % tiles_per_trip
    carry = lax.fori_loop(0, singles, lambda t, c: visit(qi - 1 - t, c, False), carry)
    first = qi - 1 - singles
    carry = lax.fori_loop(0, qi // tiles_per_trip, lambda t, c: visit_group(first - tiles_per_trip * t, c), carry)
    for h in range(heads):
        cols = slice(h * d, (h + 1) * d)
        o_ref[:, cols] = _head_rmsnorm(carry[1][h], g_ref[:, cols]).astype(o_ref.dtype)


def _sb_prompt(q, k, v, g_heads, *, t_all, mix_width, batch, seq, tile, heads_per_step):
    d = SB_HEAD_DIM
    w = heads_per_step * d
    groups = k.shape[1] // w
    nq = seq // tile
    blocks = (2 * _nbytes((tile, w), BF16) + 2 * _nbytes((seq, w), BF16) + _nbytes((8, w), F32))
    return pl.pallas_call(
        functools.partial(_sb_prompt_kernel, tile=tile, heads=heads_per_step, tiles_per_trip=SB_TILES_PER_TRIP),
        out_shape=jax.ShapeDtypeStruct((t_all, mix_width), BF16),
        grid=(batch, groups, nq),
        in_specs=[pl.BlockSpec((tile, w), lambda b, h, i: (b * nq + i, h)),
                  pl.BlockSpec((seq, w), lambda b, h, i: (b, h)),
                  pl.BlockSpec((seq, w), lambda b, h, i: (b, h)),
                  pl.BlockSpec((None, 1, w), lambda b, h, i: (h, 0, 0))],
        out_specs=pl.BlockSpec((tile, w), lambda b, h, i: (b * nq + i, h)),
        compiler_params=pltpu.CompilerParams(
            dimension_semantics=("parallel", "parallel", "arbitrary"),
            vmem_limit_bytes=_vmem_limit(blocks, temp_bytes=16 * heads_per_step * _nbytes((tile, tile), F32))),
        name="sb_prompt",
    )(q, k, v, g_heads.reshape(groups, 1, w))


def _sb_sample_kernel(q_ref, kn_ref, vn_ref, kc_ref, vc_ref, g_ref, mixed_ref, o_ref, run_ref, acc_ref,
                      *, heads, tile):
    del mixed_ref
    kc = pl.program_id(1)
    nkc = pl.num_programs(1)
    d = SB_HEAD_DIM
    n_new = kn_ref.shape[0]
    q = jnp.stack([q_ref[:, h * d:(h + 1) * d] for h in range(heads)])

    @pl.when(kc == 0)
    def _():
        pad = jnp.zeros((heads, V7X_LANES - n_new, d), BF16)
        kblk = jnp.concatenate([jnp.stack([kn_ref[:, h * d:(h + 1) * d] for h in range(heads)]).astype(BF16), pad], axis=1)
        vblk = jnp.concatenate([jnp.stack([vn_ref[:, h * d:(h + 1) * d] for h in range(heads)]).astype(BF16), pad], axis=1)
        z2, parts = _sb_scores(q, kblk, diagonal=True)
        run, acc = _sb_accumulate(z2, parts, vblk, _neg_inclusive_matrix(V7X_LANES),
                                  jnp.zeros((heads, n_new, 1), F32), jnp.zeros((heads, n_new, d), F32))
        run_ref[...] = run
        acc_ref[...] = acc

    kblk = pltpu.einshape("phd->hpd", kc_ref[...].reshape(tile, heads, d)).astype(BF16)
    vblk = pltpu.einshape("phd->hpd", vc_ref[...].reshape(tile, heads, d)).astype(BF16)
    z2, parts = _sb_scores(q, kblk, diagonal=False)
    run, acc = _sb_accumulate(z2, parts, vblk, _neg_inclusive_matrix(tile), run_ref[...], acc_ref[...])
    run_ref[...] = run
    acc_ref[...] = acc

    @pl.when(kc == nkc - 1)
    def _():
        for h in range(heads):
            cols = slice(h * d, (h + 1) * d)
            o_ref[:, cols] = _head_rmsnorm(acc_ref[h], g_ref[h:h + 1, :]).astype(o_ref.dtype)


def _sb_sample(q, k_new, v_new, k_cache, v_cache, g_heads, mixed, *, layer, row_offset, tile):
    _, batch, past, heads, d = k_cache.shape
    width = heads * d
    n_new = k_new.shape[0] // batch
    roff = row_offset // n_new
    nkc = past // tile
    blocks = (2 * _nbytes((n_new, width), BF16) + 2 * _nbytes((n_new, width), F32)
              + 2 * _nbytes((tile, heads, d), F32) + _nbytes((heads, d), F32))
    state_bytes = _nbytes((heads, n_new, V7X_LANES), F32) + _nbytes((heads, n_new, d), F32)
    k_cache = k_cache.reshape(-1, batch, past * heads, d)
    v_cache = v_cache.reshape(-1, batch, past * heads, d)
    cache_spec = pl.BlockSpec((None, None, tile * heads, d), lambda b, c: (layer, b, nkc - 1 - c, 0))
    return pl.pallas_call(
        functools.partial(_sb_sample_kernel, heads=heads, tile=tile),
        out_shape=jax.ShapeDtypeStruct(mixed.shape, mixed.dtype),
        grid=(batch, nkc),
        in_specs=[pl.BlockSpec((n_new, width), lambda b, c: (b + roff, 0)),
                  pl.BlockSpec((n_new, width), lambda b, c: (b, 0)),
                  pl.BlockSpec((n_new, width), lambda b, c: (b, 0)),
                  cache_spec,
                  cache_spec,
                  pl.BlockSpec((heads, d), lambda b, c: (0, 0)),
                  pl.BlockSpec(memory_space=pl.ANY)],
        out_specs=pl.BlockSpec((n_new, width), lambda b, c: (b + roff, 0)),
        scratch_shapes=[pltpu.VMEM((heads, n_new, 1), F32), pltpu.VMEM((heads, n_new, d), F32)],
        input_output_aliases={6: 0},
        compiler_params=pltpu.CompilerParams(
            dimension_semantics=("parallel", "arbitrary"),
            vmem_limit_bytes=_vmem_limit(blocks, scratch_bytes=state_bytes, temp_bytes=12 << 20)),
        name="sb_sample",
    )(q, k_new, v_new, k_cache, v_cache, g_heads, mixed)


def _log_sigmoid(x):
    return jnp.minimum(x, 0.0) - jnp.log1p(jnp.exp(-jnp.abs(x)))


def _gate_rows(x, chunk):
    pad = (-chunk) % V7X_LANES
    if pad:
        x = jnp.concatenate([x, jnp.zeros((pad, V7X_LANES), x.dtype)], axis=0)
    return x.T[:N_GATES, :chunk]


def _mlstm_kernel(q_ref, k_ref, v_ref, o_ref, g_ref, gain_ref, c0_ref, n0_ref, m0_ref, mixed_ref,
                  out_ref, c_out_ref, n_out_ref, m_out_ref, c_sc, n_sc, m_sc, *, chunk):
    del mixed_ref
    ci = pl.program_id(1)
    nci = pl.num_programs(1)

    @pl.when(ci == 0)
    def _():
        c_sc[...] = c0_ref[...]
        n_sc[...] = n0_ref[...]
        m_sc[...] = m0_ref[...]

    row = lax.broadcasted_iota(jnp.int32, (chunk, chunk), 0)
    col = lax.broadcasted_iota(jnp.int32, (chunk, chunk), 1)
    causal = col <= row
    tri = causal.astype(BF16)

    g = g_ref[...]
    b_cols = sum(jnp.dot(tri, part, preferred_element_type=F32) for part in _split3(_log_sigmoid(g)))
    g_rows = _gate_rows(g, chunk)
    b_rows = _gate_rows(b_cols, chunk)

    for h in range(ML_HEADS):
        qk = slice(h * ML_QK_DIM, (h + 1) * ML_QK_DIM)
        vv = slice(h * ML_V_DIM, (h + 1) * ML_V_DIM)
        q = q_ref[:, qk]
        k = k_ref[:, qk]
        v = v_ref[:, vv]
        c_prev = c_sc[h]
        n_prev = n_sc[h:h + 1, :]
        m_prev = m_sc[h:h + 1, 0:1]

        b_col = b_cols[:, ML_HEADS + h:ML_HEADS + h + 1]
        i_col = g[:, h:h + 1]
        b_row = b_rows[ML_HEADS + h:ML_HEADS + h + 1, :]
        i_row = g_rows[h:h + 1, :]

        d_log = jnp.where(causal, b_col - b_row + i_row, -jnp.inf)
        inter_log = b_col + m_prev
        m_row = jnp.maximum(inter_log, jnp.max(d_log, axis=-1, keepdims=True))
        w_intra = jnp.exp(d_log - m_row)
        w_inter = jnp.exp(inter_log - m_row)

        s = lax.dot_general(q, k, (((1,), (1,)), ((), ())), preferred_element_type=F32) * w_intra
        inter = lax.dot_general(q, c_prev.astype(BF16), (((1,), (1,)), ((), ())), preferred_element_type=F32)
        num = w_inter * inter + jnp.dot(s.astype(BF16), v, preferred_element_type=F32)
        den = w_inter * jnp.sum(q.astype(F32) * n_prev, axis=-1, keepdims=True) + jnp.sum(s, axis=-1, keepdims=True)
        hid = num / jnp.maximum(jnp.abs(den), jnp.exp(-m_row))

        m_new = m_row[chunk - 1:chunk, :]
        b_last = b_col[chunk - 1:chunk, :]
        w_prev = jnp.exp(b_last + m_prev - m_new)
        w_tok = jnp.exp(b_last - b_col + i_col - m_new)
        vw = (v.astype(F32) * w_tok).astype(BF16)
        c_sc[h] = w_prev * c_prev + lax.dot_general(vw, k, (((0,), (0,)), ((), ())), preferred_element_type=F32)
        n_sc[h:h + 1, :] = w_prev * n_prev + jnp.sum(k.astype(F32) * w_tok, axis=0, keepdims=True)
        m_sc[h:h + 1, :] = jnp.broadcast_to(m_new, (1, m_sc.shape[1]))

        normed = _head_rmsnorm(hid, gain_ref[h:h + 1, :])
        out_ref[:, vv] = (normed * _sigmoid(o_ref[:, vv])).astype(out_ref.dtype)

    @pl.when(ci == nci - 1)
    def _():
        c_out_ref[...] = c_sc[...]
        n_out_ref[...] = n_sc[...]
        m_out_ref[...] = m_sc[...]


def _mlstm(q, k, v, o, g, gain, c0, n0, m0, mixed, *, batch, seq, chunk, row_offset, col_offset):
    nci = seq // chunk
    roff = row_offset // chunk
    vw = ML_HEADS * ML_V_DIM
    qw = ML_HEADS * ML_QK_DIM
    coff = col_offset // vw
    state_bytes = (_nbytes((ML_HEADS, ML_V_DIM, ML_QK_DIM), F32) + _nbytes((ML_HEADS, ML_QK_DIM), F32)
                   + _nbytes((ML_HEADS, V7X_LANES), F32))
    blocks = (2 * _nbytes((chunk, qw), BF16) + 2 * _nbytes((chunk, vw), BF16) + _nbytes((chunk, vw), F32)
              + _nbytes((chunk, V7X_LANES), F32) + 2 * state_bytes)
    row_map = lambda b, c: (b * nci + c + roff, 0)
    state_specs = [pl.BlockSpec((None, ML_HEADS, ML_V_DIM, ML_QK_DIM), lambda b, c: (b, 0, 0, 0)),
                   pl.BlockSpec((None, ML_HEADS, ML_QK_DIM), lambda b, c: (b, 0, 0)),
                   pl.BlockSpec((None, ML_HEADS, V7X_LANES), lambda b, c: (b, 0, 0))]
    return pl.pallas_call(
        functools.partial(_mlstm_kernel, chunk=chunk),
        out_shape=[jax.ShapeDtypeStruct(mixed.shape, mixed.dtype),
                   jax.ShapeDtypeStruct(c0.shape, F32),
                   jax.ShapeDtypeStruct(n0.shape, F32),
                   jax.ShapeDtypeStruct(m0.shape, F32)],
        grid=(batch, nci),
        in_specs=[pl.BlockSpec((chunk, qw), row_map),
                  pl.BlockSpec((chunk, qw), row_map),
                  pl.BlockSpec((chunk, vw), row_map),
                  pl.BlockSpec((chunk, vw), row_map),
                  pl.BlockSpec((chunk, V7X_LANES), row_map),
                  pl.BlockSpec((ML_HEADS, ML_V_DIM), lambda b, c: (0, 0)),
                  *state_specs,
                  pl.BlockSpec(memory_space=pl.ANY)],
        out_specs=[pl.BlockSpec((chunk, vw), lambda b, c: (b * nci + c + roff, coff)), *state_specs],
        scratch_shapes=[pltpu.VMEM((ML_HEADS, ML_V_DIM, ML_QK_DIM), F32),
                        pltpu.VMEM((ML_HEADS, ML_QK_DIM), F32),
                        pltpu.VMEM((ML_HEADS, V7X_LANES), F32)],
        input_output_aliases={9: 0},
        compiler_params=pltpu.CompilerParams(
            dimension_semantics=("parallel", "arbitrary"),
            vmem_limit_bytes=_vmem_limit(blocks, scratch_bytes=state_bytes, temp_bytes=12 << 20)),
        name="mlstm",
    )(q, k, v, o, g, gain, c0, n0, m0, mixed)


def _pick_row_tile(t, target):
    best = 16
    for cand in range(16, target + 1, 16):
        if t % cand == 0:
            best = cand
    return best


def kernel(x_prompt, x_sample, cache_sb_k, cache_sb_v, state_ml_c, state_ml_n, state_ml_m, p_prompt, p_sample, ln_ffn1, w_ffn1_gate, w_ffn1_up, w_ffn1_down, ln_mix, w_in, b_if, g_sb_head, g_ml_head, w_out, ln_ffn2, w_ffn2_gate, w_ffn2_up, w_ffn2_down, ln_ple, w_ple_gate, w_ple_proj, ln_final):
    depth = w_in.shape[0]
    batch, seq, d_model = x_prompt.shape
    dec_batch, dec_seq, _ = x_sample.shape
    tp = batch * seq
    ts = dec_batch * dec_seq
    t_all = tp + ts
    sb_heads = g_sb_head.shape[1]
    sb_width = sb_heads * SB_HEAD_DIM
    ml_qk_width = ML_HEADS * ML_QK_DIM
    ml_v_width = ML_HEADS * ML_V_DIM
    mix_width = sb_width + ml_v_width
    d_ff = w_ffn1_gate.shape[2]
    off_q, off_k, off_v = 0, sb_width, 2 * sb_width
    off_mq = 3 * sb_width
    off_mk = off_mq + ml_qk_width
    off_mv = off_mk + ml_qk_width
    off_mo = off_mv + ml_v_width
    off_g = off_mo + ml_v_width

    tm_tall = _pick_row_tile(t_all, 2176)
    tm_prompt = _pick_row_tile(tp, 2176)
    norm_tile = _pick_row_tile(t_all, 384)
    ff_tn = V7X_MXU_DIM
    sb_tile = V7X_MXU_DIM
    ml_chunk = min(seq, V7X_MXU_DIM)
    tall = dict(tm=tm_tall, single_buffer_lhs=True)

    x, h = _stack_tokens(x_prompt.reshape(tp, d_model), x_sample.reshape(ts, d_model), ln_ffn1[0],
                         tile=math.gcd(tp, ts))

    def half_ffn(x, h, w_gate, w_up, w_down):
        a = _matmul_fullk(h, [w_gate, w_up], [0, 0], d_ff, [BF16], _ep_half_swiglu, tn=ff_tn,
                          name="ffn_gate_up", **tall)[0]
        return _ffn_down(a, w_down, x, tm=tm_tall, tn=1024, tk=1024)

    outs = {name: [] for name in ("pk", "pv", "pc", "pn", "pm", "sk", "sv", "sc", "sn", "sm")}
    for l in range(depth):
        if l > 0:
            h = _rmsnorm(x, ln_ffn1[l], BF16, rows=t_all, row_tile=norm_tile)
        x = half_ffn(x, h, w_ffn1_gate[l], w_ffn1_up[l], w_ffn1_down[l])

        h = _rmsnorm(x, ln_mix[l], BF16, rows=t_all, row_tile=norm_tile)
        w = jnp.swapaxes(w_in[l], 0, 1)
        proj = functools.partial(_matmul_fullk, h, w_transposed=True)
        (sb_q,) = proj([w], [off_q], sb_width, [BF16], _ep_sb_q, tn=256, name="proj_sb_q", **tall)
        k_p, v_p, k_pb, v_pb = proj([w, w], [off_k, off_v], sb_width, [F32, F32, BF16, BF16], _ep_kv,
                                    tm=tm_prompt, single_buffer_lhs=True, tn=256, rows=tp, name="proj_sb_kv_prompt")
        k_s, v_s = proj([w, w], [off_k, off_v], sb_width, [F32, F32], _ep_identity,
                        tm=ts, tn=256, rows=ts, row_offset=tp, name="proj_sb_kv_sample")
        ml_q, ml_k = proj([w, w], [off_mq, off_mk], ml_qk_width, [BF16, BF16], _ep_ml_qk,
                          tn=256, name="proj_ml_qk", **tall)
        ml_v, ml_o = proj([w, w], [off_mv, off_mo], ml_v_width, [BF16, F32], _ep_identity,
                          tn=256, name="proj_ml_vo", **tall)
        b_row = jnp.pad(b_if[l], (0, V7X_LANES - N_GATES)).reshape(1, V7X_LANES)
        (gates,) = proj([w], [off_g], V7X_LANES, [F32], _ep_bias, extras=(b_row,),
                        tn=V7X_LANES, name="proj_ml_gates", **tall)

        mixed = _sb_prompt(sb_q, k_pb, v_pb, g_sb_head[l], t_all=t_all, mix_width=mix_width, batch=batch, seq=seq,
                           tile=sb_tile, heads_per_step=8)
        mixed = _sb_sample(sb_q, k_s, v_s, cache_sb_k, cache_sb_v, g_sb_head[l], mixed,
                           layer=l, row_offset=tp, tile=sb_tile)
        c0 = jnp.zeros((batch, ML_HEADS, ML_V_DIM, ML_QK_DIM), F32)
        n0 = jnp.zeros((batch, ML_HEADS, ML_QK_DIM), F32)
        m0 = jnp.zeros((batch, ML_HEADS, V7X_LANES), F32)
        mixed, c_p, n_p, m_p = _mlstm(ml_q, ml_k, ml_v, ml_o, gates, g_ml_head[l], c0, n0, m0, mixed,
                                      batch=batch, seq=seq, chunk=ml_chunk, row_offset=0, col_offset=sb_width)
        m0_s = jnp.broadcast_to(state_ml_m[l][:, :, None], (dec_batch, ML_HEADS, V7X_LANES))
        mixed, c_s, n_s, m_s = _mlstm(ml_q, ml_k, ml_v, ml_o, gates, g_ml_head[l], state_ml_c[l], state_ml_n[l],
                                      m0_s, mixed, batch=dec_batch, seq=dec_seq, chunk=dec_seq,
                                      row_offset=tp, col_offset=sb_width)

        (x,) = _matmul_fullk(mixed, [w_out[l]], [0], d_model, [F32], _ep_residual, extras=(x,),
                             tn=256, name="out_proj", **tall)
        h = _rmsnorm(x, ln_ffn2[l], BF16, rows=t_all, row_tile=norm_tile)
        x = half_ffn(x, h, w_ffn2_gate[l], w_ffn2_up[l], w_ffn2_down[l])

        h = _rmsnorm(x, ln_ple[l], BF16, rows=t_all, row_tile=norm_tile)
        p = jnp.concatenate([p_prompt[l].reshape(tp, -1), p_sample[l].reshape(ts, -1)], axis=0)
        x = _ple(h, w_ple_gate[l], p, w_ple_proj[l], x, tm=tm_tall, tn=256)

        outs["pk"].append(k_p.reshape(batch, seq, sb_heads, SB_HEAD_DIM))
        outs["pv"].append(v_p.reshape(batch, seq, sb_heads, SB_HEAD_DIM))
        outs["pc"].append(c_p)
        outs["pn"].append(n_p)
        outs["pm"].append(m_p[:, :, 0])
        outs["sk"].append(k_s.reshape(dec_batch, dec_seq, sb_heads, SB_HEAD_DIM))
        outs["sv"].append(v_s.reshape(dec_batch, dec_seq, sb_heads, SB_HEAD_DIM))
        outs["sc"].append(c_s)
        outs["sn"].append(n_s)
        outs["sm"].append(m_s[:, :, 0])

    y_prompt = _rmsnorm(x, ln_final, F32, rows=tp, row_tile=256).reshape(batch, seq, d_model)
    y_sample = _rmsnorm(x, ln_final, F32, rows=ts, row_tile=ts, row_offset=tp).reshape(dec_batch, dec_seq, d_model)

    def stack(name):
        vals = outs[name]
        return vals[0][None] if len(vals) == 1 else jnp.stack(vals)

    return (y_prompt, y_sample, stack("pk"), stack("pv"), stack("pc"), stack("pn"), stack("pm"),
            stack("sk"), stack("sv"), stack("sc"), stack("sn"), stack("sm"))
```

```python
import functools
import math

import jax
import jax.numpy as jnp
from jax import lax
from jax.experimental import pallas as pl
from jax.experimental.pallas import tpu as pltpu

F32 = jnp.float32
BF16 = jnp.bfloat16
EPS = 1e-6
LOG2_E = math.log2(math.e)

V7X_VMEM_BYTES = 64 * 1024 * 1024
V7X_LANES = 128
V7X_MXU_DIM = 256
VMEM_REQUEST_CAP = V7X_VMEM_BYTES - 6 * 1024 * 1024

SB_HEAD_DIM = 128
ML_QK_DIM = 256
ML_V_DIM = 512
ML_HEADS = 4
N_GATES = 2 * ML_HEADS
SB_Q_SCALE = LOG2_E * SB_HEAD_DIM ** -0.5
SB_TILES_PER_TRIP = 4


def _vmem_limit(block_bytes, scratch_bytes=0, temp_bytes=0):
    est = 2 * block_bytes + scratch_bytes + temp_bytes + (2 << 20)
    return int(min(max(est, 16 << 20), VMEM_REQUEST_CAP))


def _nbytes(shape, dtype):
    n = 1
    for s in shape:
        n *= s
    return n * jnp.dtype(dtype).itemsize


def _sigmoid(x):
    return 1.0 / (1.0 + jnp.exp(-x))


def _rmsnorm_kernel(x_ref, g_ref, o_ref):
    x = x_ref[...]
    y = x * lax.rsqrt(jnp.mean(x * x, axis=-1, keepdims=True) + EPS)
    o_ref[...] = (y * g_ref[...]).astype(o_ref.dtype)


def _rmsnorm(x, g, out_dtype, *, rows, row_tile, row_offset=0):
    d = x.shape[1]
    off = row_offset // row_tile
    blocks = _nbytes((row_tile, d), F32) + _nbytes((row_tile, d), out_dtype)
    return pl.pallas_call(
        _rmsnorm_kernel,
        out_shape=jax.ShapeDtypeStruct((rows, d), out_dtype),
        grid=(rows // row_tile,),
        in_specs=[pl.BlockSpec((row_tile, d), lambda i: (i + off, 0)),
                  pl.BlockSpec((1, d), lambda i: (0, 0))],
        out_specs=pl.BlockSpec((row_tile, d), lambda i: (i, 0)),
        compiler_params=pltpu.CompilerParams(
            dimension_semantics=("parallel",),
            vmem_limit_bytes=_vmem_limit(blocks, temp_bytes=2 * _nbytes((row_tile, d), F32))),
        name="rmsnorm",
    )(x, g.reshape(1, d))


def _stack_kernel(xp_ref, xs_ref, g_ref, x_ref, h_ref, *, n_prompt_tiles):
    i = pl.program_id(0)

    def emit(src_ref):
        x_ref[...] = src_ref[...]
        _rmsnorm_kernel(src_ref, g_ref, h_ref)

    pl.when(i < n_prompt_tiles)(lambda: emit(xp_ref))
    pl.when(i >= n_prompt_tiles)(lambda: emit(xs_ref))


def _stack_tokens(x_prompt, x_sample, gain, *, tile):
    tp, d = x_prompt.shape
    ts = x_sample.shape[0]
    n_p, n_s = tp // tile, ts // tile
    t_all = tp + ts
    blocks = 3 * _nbytes((tile, d), F32) + _nbytes((tile, d), BF16)
    row = lambda i: (i, 0)
    return pl.pallas_call(
        functools.partial(_stack_kernel, n_prompt_tiles=n_p),
        out_shape=[jax.ShapeDtypeStruct((t_all, d), F32), jax.ShapeDtypeStruct((t_all, d), BF16)],
        grid=(n_p + n_s,),
        in_specs=[pl.BlockSpec((tile, d), lambda i: (jnp.minimum(i, n_p - 1), 0)),
                  pl.BlockSpec((tile, d), lambda i: (jnp.maximum(i - n_p, 0), 0)),
                  pl.BlockSpec((1, d), lambda i: (0, 0))],
        out_specs=[pl.BlockSpec((tile, d), row), pl.BlockSpec((tile, d), row)],
        compiler_params=pltpu.CompilerParams(
            dimension_semantics=("parallel",),
            vmem_limit_bytes=_vmem_limit(blocks, temp_bytes=2 * _nbytes((tile, d), F32))),
        name="stack_tokens",
    )(x_prompt, x_sample, gain.reshape(1, d))


def _mm_kernel(*refs, n_w, n_extra, epilogue, w_transposed):
    lhs_ref = refs[0]
    w_refs = refs[1:1 + n_w]
    extra_refs = refs[1 + n_w:1 + n_w + n_extra]
    out_refs = refs[1 + n_w + n_extra:]
    lhs = lhs_ref[...]
    w_contract = 1 if w_transposed else 0
    accs = [lax.dot_general(lhs, w[...].astype(BF16), (((1,), (w_contract,)), ((), ())),
                            preferred_element_type=F32) for w in w_refs]
    outs = epilogue(accs, [e[...] for e in extra_refs])
    for o_ref, o in zip(out_refs, outs):
        o_ref[...] = o.astype(o_ref.dtype)


def _matmul_fullk(lhs, ws, col_offsets, n_cols, out_dtypes, epilogue, extras=(), *,
                  tm, tn, rows=None, row_offset=0, w_transposed=False, single_buffer_lhs=False, name):
    k_dim = lhs.shape[1]
    rows = lhs.shape[0] if rows is None else rows
    roff = row_offset // tm
    lhs_mode = dict(pipeline_mode=pl.Buffered(1)) if single_buffer_lhs else {}
    in_specs = [pl.BlockSpec((tm, k_dim), lambda i, j: (i + roff, 0), **lhs_mode)]
    for off in col_offsets:
        if w_transposed:
            in_specs.append(pl.BlockSpec((tn, k_dim), lambda i, j, o=off // tn: (j + o, 0)))
        else:
            in_specs.append(pl.BlockSpec((k_dim, tn), lambda i, j, o=off // tn: (0, j + o)))
    extra_bytes = 0
    for e in extras:
        if e.shape[0] == 1:
            in_specs.append(pl.BlockSpec((1, tn), lambda i, j: (0, j)))
            extra_bytes += _nbytes((8, tn), F32)
        else:
            in_specs.append(pl.BlockSpec((tm, tn), lambda i, j: (i + roff, j)))
            extra_bytes += _nbytes((tm, tn), F32)
    out_specs = [pl.BlockSpec((tm, tn), lambda i, j: (i, j)) for _ in out_dtypes]
    out_shape = [jax.ShapeDtypeStruct((rows, n_cols), dt) for dt in out_dtypes]
    out_bytes = sum(_nbytes((tm, tn), dt) for dt in out_dtypes)
    blocks = _nbytes((tm, k_dim), BF16) + len(ws) * _nbytes((k_dim, tn), F32) + extra_bytes + out_bytes
    temps = (len(ws) * ((3 if w_transposed else 1) * _nbytes((k_dim, tn), BF16) + 4 * _nbytes((tm, tn), F32))
             + len(extras) * _nbytes((tm, tn), F32))
    if single_buffer_lhs:
        temps -= _nbytes((tm, k_dim), BF16)
    return pl.pallas_call(
        functools.partial(_mm_kernel, n_w=len(ws), n_extra=len(extras), epilogue=epilogue,
                          w_transposed=w_transposed),
        out_shape=out_shape,
        grid=(rows // tm, n_cols // tn),
        in_specs=in_specs,
        out_specs=out_specs,
        compiler_params=pltpu.CompilerParams(
            dimension_semantics=("parallel", "arbitrary"),
            vmem_limit_bytes=_vmem_limit(blocks, temp_bytes=temps)),
        name=name,
    )(lhs, *ws, *extras)


def _ep_half_swiglu(accs, extras):
    g, u = accs
    return [0.5 * (g * _sigmoid(g) * u)]


def _ep_identity(accs, extras):
    return list(accs)


def _ep_kv(accs, extras):
    k, v = accs
    return [k, v, k, v]


def _ep_residual(accs, extras):
    return [extras[0] + accs[0]]


def _ep_bias(accs, extras):
    return [accs[0] + extras[0]]


def _ep_sb_q(accs, extras):
    return [accs[0] * SB_Q_SCALE]


def _ep_ml_qk(accs, extras):
    q, k = accs
    return [q * (ML_QK_DIM ** -0.5), k]


def _down_kernel(a_ref, w_ref, x_ref, o_ref, *, nk, k_valid_last, tk):
    k = pl.program_id(2)

    def partial_product(masked):
        a = a_ref[...]
        w = w_ref[...]
        if masked:
            a = jnp.where(lax.broadcasted_iota(jnp.int32, a.shape, 1) < k_valid_last, a, jnp.zeros_like(a))
            w = jnp.where(lax.broadcasted_iota(jnp.int32, w.shape, 0) < k_valid_last, w, jnp.zeros_like(w))
        return jnp.dot(a, w.astype(BF16), preferred_element_type=F32)

    @pl.when(k == 0)
    def _():
        o_ref[...] = x_ref[...] + partial_product(nk == 1 and k_valid_last != tk)

    if nk > 1:
        if k_valid_last == tk:
            @pl.when(k > 0)
            def _():
                o_ref[...] += partial_product(False)
        else:
            @pl.when(jnp.logical_and(k > 0, k < nk - 1))
            def _():
                o_ref[...] += partial_product(False)

            @pl.when(k == nk - 1)
            def _():
                o_ref[...] += partial_product(True)


def _ffn_down(a, w_down, x, *, tm, tn, tk):
    t, f = a.shape
    d = w_down.shape[1]
    nk = pl.cdiv(f, tk)
    k_valid_last = f - (nk - 1) * tk
    blocks = (_nbytes((tm, tk), BF16) + _nbytes((tk, tn), F32) + 2 * _nbytes((tm, tn), F32))
    ni, nj = t // tm, d // tn

    def residual_tile(i, j, k):
        nxt = jnp.minimum(i * nj + j + (k >= nk // 2).astype(jnp.int32), ni * nj - 1)
        return (nxt // nj, nxt % nj)

    return pl.pallas_call(
        functools.partial(_down_kernel, nk=nk, k_valid_last=k_valid_last, tk=tk),
        out_shape=jax.ShapeDtypeStruct((t, d), F32),
        grid=(ni, nj, nk),
        in_specs=[pl.BlockSpec((tm, tk), lambda i, j, k: (i, k)),
                  pl.BlockSpec((tk, tn), lambda i, j, k: (k, j)),
                  pl.BlockSpec((tm, tn), residual_tile if nk > 1 else (lambda i, j, k: (i, j)))],
        out_specs=pl.BlockSpec((tm, tn), lambda i, j, k: (i, j)),
        compiler_params=pltpu.CompilerParams(
            dimension_semantics=("arbitrary", "arbitrary", "arbitrary"),
            vmem_limit_bytes=_vmem_limit(blocks, temp_bytes=_nbytes((tk, tn), BF16) + (4 << 20))),
        name="ffn_down",
    )(a, w_down, x)


def _ple_kernel(h_ref, wg_ref, p_ref, wp_ref, x_ref, o_ref):
    gate = _sigmoid(jnp.dot(h_ref[...], wg_ref[...].astype(BF16), preferred_element_type=F32))
    proj = jnp.dot(p_ref[...].astype(BF16), wp_ref[...].astype(BF16), preferred_element_type=F32)
    o_ref[...] = x_ref[...] + gate * proj


def _ple(h, w_gate, p, w_proj, x, *, tm, tn):
    t, d = h.shape
    pd = p.shape[1]
    row_blocks = _nbytes((tm, d), BF16) + _nbytes((tm, pd), F32)
    blocks = _nbytes((d, tn), F32) + _nbytes((pd, tn), F32) + 2 * _nbytes((tm, tn), F32)
    once = dict(pipeline_mode=pl.Buffered(1))
    return pl.pallas_call(
        _ple_kernel,
        out_shape=jax.ShapeDtypeStruct((t, d), F32),
        grid=(t // tm, d // tn),
        in_specs=[pl.BlockSpec((tm, d), lambda i, j: (i, 0), **once),
                  pl.BlockSpec((d, tn), lambda i, j: (0, j)),
                  pl.BlockSpec((tm, pd), lambda i, j: (i, 0), **once),
                  pl.BlockSpec((pd, tn), lambda i, j: (0, j)),
                  pl.BlockSpec((tm, tn), lambda i, j: (i, j))],
        out_specs=pl.BlockSpec((tm, tn), lambda i, j: (i, j)),
        compiler_params=pltpu.CompilerParams(
            dimension_semantics=("parallel", "arbitrary"),
            vmem_limit_bytes=_vmem_limit(blocks, scratch_bytes=row_blocks,
                                         temp_bytes=_nbytes((d, tn), BF16) + 3 * _nbytes((tm, tn), F32))),
        name="ple",
    )(h, w_gate, p, w_proj, x)


def _split3(x):
    hi = x.astype(BF16)
    r = x - hi.astype(F32)
    mid = r.astype(BF16)
    lo = (r - mid.astype(F32)).astype(BF16)
    return hi, mid, lo


def _neg_inclusive_matrix(tk):
    r = lax.broadcasted_iota(jnp.int32, (2 * tk, tk), 0)
    c = lax.broadcasted_iota(jnp.int32, (2 * tk, tk), 1)
    r = jnp.where(r >= tk, r - tk, r)
    return jnp.where(r >= c, -1.0, 0.0).astype(BF16)


def _neg_abs(x):
    bits = lax.bitcast_convert_type(x, jnp.uint32) | jnp.uint32(0x80000000)
    return lax.bitcast_convert_type(bits, F32)


def _sb_scores(q, kblk, *, diagonal):
    z2 = lax.dot_general(q, kblk, (((2,), (2,)), ((0,), (0,))), preferred_element_type=F32)
    heads, tq, tk = z2.shape
    sp2 = jnp.maximum(z2, 0.0) + jnp.log2(1.0 + jnp.exp2(_neg_abs(z2)))
    if diagonal:
        row = lax.broadcasted_iota(jnp.int32, (1, tq, tk), 1)
        col = lax.broadcasted_iota(jnp.int32, (1, tq, tk), 2)
        mask = col < row
        sp2 = jnp.where(mask, sp2, 0.0)
        z2 = jnp.where(mask, z2, -jnp.inf)
    hi = sp2.astype(BF16)
    lo = (sp2 - hi.astype(F32)).astype(BF16)
    parts = jnp.concatenate([hi, lo], axis=-1).reshape(heads * tq, 2 * tk)
    return z2, parts


def _sb_accumulate(z2, parts, vblk, neg_incl, run, acc):
    heads, tq, tk = z2.shape
    incl = jnp.dot(parts, neg_incl, preferred_element_type=F32).reshape(heads, tq, tk)
    a = jnp.exp2((z2 + run) + incl)
    acc = acc + lax.dot_general(a.astype(BF16), vblk, (((2,), (1,)), ((0,), (0,))), preferred_element_type=F32)
    run = run + incl[:, :, 0:1]
    return run, acc


def _head_rmsnorm(x, g):
    return x * lax.rsqrt(jnp.mean(x * x, axis=-1, keepdims=True) + EPS) * g


def _sb_prompt_kernel(q_ref, k_ref, v_ref, g_ref, o_ref, *, tile, heads, tiles_per_trip):
    qi = pl.program_id(2)
    d = SB_HEAD_DIM
    neg_incl = _neg_inclusive_matrix(tile)
    q = jnp.stack([q_ref[:, h * d:(h + 1) * d] for h in range(heads)])

    def key_rows(kb):
        return pl.ds(pl.multiple_of(kb * tile, tile), tile)

    def visit(kb, carry, diagonal):
        kblk = jnp.stack([k_ref[key_rows(kb), h * d:(h + 1) * d] for h in range(heads)])
        vblk = jnp.stack([v_ref[key_rows(kb), h * d:(h + 1) * d] for h in range(heads)])
        z2, parts = _sb_scores(q, kblk, diagonal=diagonal)
        return _sb_accumulate(z2, parts, vblk, neg_incl, *carry)

    def visit_group(kb, carry):
        for g in range(tiles_per_trip):
            carry = visit(kb - g, carry, False)
        return carry

    carry = (jnp.zeros((heads, tile, 1), F32), jnp.zeros((heads, tile, d), F32))
    carry = visit(qi, carry, True)
    singles = qi % tiles_per_trip
    carry = lax.fori_loop(0, singles, lambda t, c: visit(qi - 1 - t, c, False), carry)
    first = qi - 1 - singles
    carry = lax.fori_loop(0, qi // tiles_per_trip, lambda t, c: visit_group(first - tiles_per_trip * t, c), carry)
    for h in range(heads):
        cols = slice(h * d, (h + 1) * d)
        o_ref[:, cols] = _head_rmsnorm(carry[1][h], g_ref[:, cols]).astype(o_ref.dtype)


def _sb_prompt(q, k, v, g_heads, *, t_all, mix_width, batch, seq, tile, heads_per_step):
    d = SB_HEAD_DIM
    w = heads_per_step * d
    groups = k.shape[1] // w
    nq = seq // tile
    blocks = (2 * _nbytes((tile, w), BF16) + 2 * _nbytes((seq, w), BF16) + _nbytes((8, w), F32))
    return pl.pallas_call(
        functools.partial(_sb_prompt_kernel, tile=tile, heads=heads_per_step, tiles_per_trip=SB_TILES_PER_TRIP),
        out_shape=jax.ShapeDtypeStruct((t_all, mix_width), BF16),
        grid=(batch, groups, nq),
        in_specs=[pl.BlockSpec((tile, w), lambda b, h, i: (b * nq + i, h)),
                  pl.BlockSpec((seq, w), lambda b, h, i: (b, h)),
                  pl.BlockSpec((seq, w), lambda b, h, i: (b, h)),
                  pl.BlockSpec((None, 1, w), lambda b, h, i: (h, 0, 0))],
        out_specs=pl.BlockSpec((tile, w), lambda b, h, i: (b * nq + i, h)),
        compiler_params=pltpu.CompilerParams(
            dimension_semantics=("parallel", "parallel", "arbitrary"),
            vmem_limit_bytes=_vmem_limit(blocks, temp_bytes=16 * heads_per_step * _nbytes((tile, tile), F32))),
        name="sb_prompt",
    )(q, k, v, g_heads.reshape(groups, 1, w))


def _sb_sample_kernel(q_ref, kn_ref, vn_ref, kc_ref, vc_ref, g_ref, mixed_ref, o_ref, run_ref, acc_ref,
                      *, heads, tile):
    del mixed_ref
    kc = pl.program_id(1)
    nkc = pl.num_programs(1)
    d = SB_HEAD_DIM
    n_new = kn_ref.shape[0]
    q = jnp.stack([q_ref[:, h * d:(h + 1) * d] for h in range(heads)])

    @pl.when(kc == 0)
    def _():
        pad = jnp.zeros((heads, V7X_LANES - n_new, d), BF16)
        kblk = jnp.concatenate([jnp.stack([kn_ref[:, h * d:(h + 1) * d] for h in range(heads)]).astype(BF16), pad], axis=1)
        vblk = jnp.concatenate([jnp.stack([vn_ref[:, h * d:(h + 1) * d] for h in range(heads)]).astype(BF16), pad], axis=1)
        z2, parts = _sb_scores(q, kblk, diagonal=True)
        run, acc = _sb_accumulate(z2, parts, vblk, _neg_inclusive_matrix(V7X_LANES),
                                  jnp.zeros((heads, n_new, 1), F32), jnp.zeros((heads, n_new, d), F32))
        run_ref[...] = run
        acc_ref[...] = acc

    kblk = pltpu.einshape("phd->hpd", kc_ref[...].astype(BF16).reshape(tile, heads, d))
    vblk = pltpu.einshape("phd->hpd", vc_ref[...].astype(BF16).reshape(tile, heads, d))
    z2, parts = _sb_scores(q, kblk, diagonal=False)
    run, acc = _sb_accumulate(z2, parts, vblk, _neg_inclusive_matrix(tile), run_ref[...], acc_ref[...])
    run_ref[...] = run
    acc_ref[...] = acc

    @pl.when(kc == nkc - 1)
    def _():
        for h in range(heads):
            cols = slice(h * d, (h + 1) * d)
            o_ref[:, cols] = _head_rmsnorm(acc_ref[h], g_ref[h:h + 1, :]).astype(o_ref.dtype)


def _sb_sample(q, k_new, v_new, k_cache, v_cache, g_heads, mixed, *, layer, row_offset, tile):
    _, batch, past, heads, d = k_cache.shape
    width = heads * d
    n_new = k_new.shape[0] // batch
    roff = row_offset // n_new
    nkc = past // tile
    blocks = (2 * _nbytes((n_new, width), BF16) + 2 * _nbytes((n_new, width), F32)
              + 2 * _nbytes((tile, heads, d), F32) + _nbytes((heads, d), F32))
    state_bytes = _nbytes((heads, n_new, V7X_LANES), F32) + _nbytes((heads, n_new, d), F32)
    k_cache = k_cache.reshape(-1, batch, past * heads, d)
    v_cache = v_cache.reshape(-1, batch, past * heads, d)
    cache_spec = pl.BlockSpec((None, None, tile * heads, d), lambda b, c: (layer, b, nkc - 1 - c, 0))
    return pl.pallas_call(
        functools.partial(_sb_sample_kernel, heads=heads, tile=tile),
        out_shape=jax.ShapeDtypeStruct(mixed.shape, mixed.dtype),
        grid=(batch, nkc),
        in_specs=[pl.BlockSpec((n_new, width), lambda b, c: (b + roff, 0)),
                  pl.BlockSpec((n_new, width), lambda b, c: (b, 0)),
                  pl.BlockSpec((n_new, width), lambda b, c: (b, 0)),
                  cache_spec,
                  cache_spec,
                  pl.BlockSpec((heads, d), lambda b, c: (0, 0)),
                  pl.BlockSpec(memory_space=pl.ANY)],
        out_specs=pl.BlockSpec((n_new, width), lambda b, c: (b + roff, 0)),
        scratch_shapes=[pltpu.VMEM((heads, n_new, 1), F32), pltpu.VMEM((heads, n_new, d), F32)],
        input_output_aliases={6: 0},
        compiler_params=pltpu.CompilerParams(
            dimension_semantics=("parallel", "arbitrary"),
            vmem_limit_bytes=_vmem_limit(blocks, scratch_bytes=state_bytes, temp_bytes=12 << 20)),
        name="sb_sample",
    )(q, k_new, v_new, k_cache, v_cache, g_heads, mixed)


def _log_sigmoid(x):
    return jnp.minimum(x, 0.0) - jnp.log1p(jnp.exp(-jnp.abs(x)))


def _gate_rows(x, chunk):
    pad = (-chunk) % V7X_LANES
    if pad:
        x = jnp.concatenate([x, jnp.zeros((pad, V7X_LANES), x.dtype)], axis=0)
    return x.T[:N_GATES, :chunk]


def _mlstm_kernel(q_ref, k_ref, v_ref, o_ref, g_ref, gain_ref, c0_ref, n0_ref, m0_ref, mixed_ref,
                  out_ref, c_out_ref, n_out_ref, m_out_ref, c_sc, n_sc, m_sc, *, chunk):
    del mixed_ref
    ci = pl.program_id(1)
    nci = pl.num_programs(1)

    @pl.when(ci == 0)
    def _():
        c_sc[...] = c0_ref[...]
        n_sc[...] = n0_ref[...]
        m_sc[...] = m0_ref[...]

    row = lax.broadcasted_iota(jnp.int32, (chunk, chunk), 0)
    col = lax.broadcasted_iota(jnp.int32, (chunk, chunk), 1)
    causal = col <= row
    tri = causal.astype(BF16)

    g = g_ref[...]
    b_cols = sum(jnp.dot(tri, part, preferred_element_type=F32) for part in _split3(_log_sigmoid(g)))
    g_rows = _gate_rows(g, chunk)
    b_rows = _gate_rows(b_cols, chunk)

    for h in range(ML_HEADS):
        qk = slice(h * ML_QK_DIM, (h + 1) * ML_QK_DIM)
        vv = slice(h * ML_V_DIM, (h + 1) * ML_V_DIM)
        q = q_ref[:, qk]
        k = k_ref[:, qk]
        v = v_ref[:, vv]
        c_prev = c_sc[h]
        n_prev = n_sc[h:h + 1, :]
        m_prev = m_sc[h:h + 1, 0:1]

        b_col = b_cols[:, ML_HEADS + h:ML_HEADS + h + 1]
        i_col = g[:, h:h + 1]
        b_row = b_rows[ML_HEADS + h:ML_HEADS + h + 1, :]
        i_row = g_rows[h:h + 1, :]

        d_log = jnp.where(causal, b_col - b_row + i_row, -jnp.inf)
        inter_log = b_col + m_prev
        m_row = jnp.maximum(inter_log, jnp.max(d_log, axis=-1, keepdims=True))
        w_intra = jnp.exp(d_log - m_row)
        w_inter = jnp.exp(inter_log - m_row)

        s = lax.dot_general(q, k, (((1,), (1,)), ((), ())), preferred_element_type=F32) * w_intra
        inter = lax.dot_general(q, c_prev.astype(BF16), (((1,), (1,)), ((), ())), preferred_element_type=F32)
        num = w_inter * inter + jnp.dot(s.astype(BF16), v, preferred_element_type=F32)
        den = w_inter * jnp.sum(q.astype(F32) * n_prev, axis=-1, keepdims=True) + jnp.sum(s, axis=-1, keepdims=True)
        hid = num / jnp.maximum(jnp.abs(den), jnp.exp(-m_row))

        m_new = m_row[chunk - 1:chunk, :]
        b_last = b_col[chunk - 1:chunk, :]
        w_prev = jnp.exp(b_last + m_prev - m_new)
        w_tok = jnp.exp(b_last - b_col + i_col - m_new)
        vw = (v.astype(F32) * w_tok).astype(BF16)
        c_sc[h] = w_prev * c_prev + lax.dot_general(vw, k, (((0,), (0,)), ((), ())), preferred_element_type=F32)
        n_sc[h:h + 1, :] = w_prev * n_prev + jnp.sum(k.astype(F32) * w_tok, axis=0, keepdims=True)
        m_sc[h:h + 1, :] = jnp.broadcast_to(m_new, (1, m_sc.shape[1]))

        normed = _head_rmsnorm(hid, gain_ref[h:h + 1, :])
        out_ref[:, vv] = (normed * _sigmoid(o_ref[:, vv])).astype(out_ref.dtype)

    @pl.when(ci == nci - 1)
    def _():
        c_out_ref[...] = c_sc[...]
        n_out_ref[...] = n_sc[...]
        m_out_ref[...] = m_sc[...]


def _mlstm(q, k, v, o, g, gain, c0, n0, m0, mixed, *, batch, seq, chunk, row_offset, col_offset):
    nci = seq // chunk
    roff = row_offset // chunk
    vw = ML_HEADS * ML_V_DIM
    qw = ML_HEADS * ML_QK_DIM
    coff = col_offset // vw
    state_bytes = (_nbytes((ML_HEADS, ML_V_DIM, ML_QK_DIM), F32) + _nbytes((ML_HEADS, ML_QK_DIM), F32)
                   + _nbytes((ML_HEADS, V7X_LANES), F32))
    blocks = (2 * _nbytes((chunk, qw), BF16) + 2 * _nbytes((chunk, vw), BF16) + _nbytes((chunk, vw), F32)
              + _nbytes((chunk, V7X_LANES), F32) + 2 * state_bytes)
    row_map = lambda b, c: (b * nci + c + roff, 0)
    state_specs = [pl.BlockSpec((None, ML_HEADS, ML_V_DIM, ML_QK_DIM), lambda b, c: (b, 0, 0, 0)),
                   pl.BlockSpec((None, ML_HEADS, ML_QK_DIM), lambda b, c: (b, 0, 0)),
                   pl.BlockSpec((None, ML_HEADS, V7X_LANES), lambda b, c: (b, 0, 0))]
    return pl.pallas_call(
        functools.partial(_mlstm_kernel, chunk=chunk),
        out_shape=[jax.ShapeDtypeStruct(mixed.shape, mixed.dtype),
                   jax.ShapeDtypeStruct(c0.shape, F32),
                   jax.ShapeDtypeStruct(n0.shape, F32),
                   jax.ShapeDtypeStruct(m0.shape, F32)],
        grid=(batch, nci),
        in_specs=[pl.BlockSpec((chunk, qw), row_map),
                  pl.BlockSpec((chunk, qw), row_map),
                  pl.BlockSpec((chunk, vw), row_map),
                  pl.BlockSpec((chunk, vw), row_map),
                  pl.BlockSpec((chunk, V7X_LANES), row_map),
                  pl.BlockSpec((ML_HEADS, ML_V_DIM), lambda b, c: (0, 0)),
                  *state_specs,
                  pl.BlockSpec(memory_space=pl.ANY)],
        out_specs=[pl.BlockSpec((chunk, vw), lambda b, c: (b * nci + c + roff, coff)), *state_specs],
        scratch_shapes=[pltpu.VMEM((ML_HEADS, ML_V_DIM, ML_QK_DIM), F32),
                        pltpu.VMEM((ML_HEADS, ML_QK_DIM), F32),
                        pltpu.VMEM((ML_HEADS, V7X_LANES), F32)],
        input_output_aliases={9: 0},
        compiler_params=pltpu.CompilerParams(
            dimension_semantics=("parallel", "arbitrary"),
            vmem_limit_bytes=_vmem_limit(blocks, scratch_bytes=state_bytes, temp_bytes=12 << 20)),
        name="mlstm",
    )(q, k, v, o, g, gain, c0, n0, m0, mixed)


def _pick_row_tile(t, target):
    best = 16
    for cand in range(16, target + 1, 16):
        if t % cand == 0:
            best = cand
    return best


def kernel(x_prompt, x_sample, cache_sb_k, cache_sb_v, state_ml_c, state_ml_n, state_ml_m, p_prompt, p_sample, ln_ffn1, w_ffn1_gate, w_ffn1_up, w_ffn1_down, ln_mix, w_in, b_if, g_sb_head, g_ml_head, w_out, ln_ffn2, w_ffn2_gate, w_ffn2_up, w_ffn2_down, ln_ple, w_ple_gate, w_ple_proj, ln_final):
    depth = w_in.shape[0]
    batch, seq, d_model = x_prompt.shape
    dec_batch, dec_seq, _ = x_sample.shape
    tp = batch * seq
    ts = dec_batch * dec_seq
    t_all = tp + ts
    sb_heads = g_sb_head.shape[1]
    sb_width = sb_heads * SB_HEAD_DIM
    ml_qk_width = ML_HEADS * ML_QK_DIM
    ml_v_width = ML_HEADS * ML_V_DIM
    mix_width = sb_width + ml_v_width
    d_ff = w_ffn1_gate.shape[2]
    off_q, off_k, off_v = 0, sb_width, 2 * sb_width
    off_mq = 3 * sb_width
    off_mk = off_mq + ml_qk_width
    off_mv = off_mk + ml_qk_width
    off_mo = off_mv + ml_v_width
    off_g = off_mo + ml_v_width

    tm_tall = _pick_row_tile(t_all, 2176)
    tm_prompt = _pick_row_tile(tp, 2176)
    norm_tile = _pick_row_tile(t_all, 384)
    ff_tn = V7X_MXU_DIM
    sb_tile = V7X_MXU_DIM
    ml_chunk = min(seq, V7X_MXU_DIM)
    tall = dict(tm=tm_tall, single_buffer_lhs=True)

    x, h = _stack_tokens(x_prompt.reshape(tp, d_model), x_sample.reshape(ts, d_model), ln_ffn1[0],
                         tile=math.gcd(tp, ts))

    def half_ffn(x, h, w_gate, w_up, w_down):
        a = _matmul_fullk(h, [w_gate, w_up], [0, 0], d_ff, [BF16], _ep_half_swiglu, tn=ff_tn,
                          name="ffn_gate_up", **tall)[0]
        return _ffn_down(a, w_down, x, tm=tm_tall, tn=1024, tk=1024)

    outs = {name: [] for name in ("pk", "pv", "pc", "pn", "pm", "sk", "sv", "sc", "sn", "sm")}
    for l in range(depth):
        if l > 0:
            h = _rmsnorm(x, ln_ffn1[l], BF16, rows=t_all, row_tile=norm_tile)
        x = half_ffn(x, h, w_ffn1_gate[l], w_ffn1_up[l], w_ffn1_down[l])

        h = _rmsnorm(x, ln_mix[l], BF16, rows=t_all, row_tile=norm_tile)
        w = jnp.swapaxes(w_in[l], 0, 1)
        proj = functools.partial(_matmul_fullk, h, w_transposed=True)
        (sb_q,) = proj([w], [off_q], sb_width, [BF16], _ep_sb_q, tm=tm_tall, tn=256, name="proj_sb_q")
        k_p, v_p, k_pb, v_pb = proj([w, w], [off_k, off_v], sb_width, [F32, F32, BF16, BF16], _ep_kv,
                                    tm=tm_prompt, single_buffer_lhs=True, tn=256, rows=tp, name="proj_sb_kv_prompt")
        k_s, v_s = proj([w, w], [off_k, off_v], sb_width, [F32, F32], _ep_identity,
                        tm=ts, tn=256, rows=ts, row_offset=tp, name="proj_sb_kv_sample")
        ml_q, ml_k = proj([w, w], [off_mq, off_mk], ml_qk_width, [BF16, BF16], _ep_ml_qk,
                          tn=256, name="proj_ml_qk", **tall)
        ml_v, ml_o = proj([w, w], [off_mv, off_mo], ml_v_width, [BF16, F32], _ep_identity,
                          tn=256, name="proj_ml_vo", **tall)
        b_row = jnp.pad(b_if[l], (0, V7X_LANES - N_GATES)).reshape(1, V7X_LANES)
        (gates,) = proj([w], [off_g], V7X_LANES, [F32], _ep_bias, extras=(b_row,),
                        tm=tm_tall, tn=V7X_LANES, name="proj_ml_gates")

        mixed = _sb_prompt(sb_q, k_pb, v_pb, g_sb_head[l], t_all=t_all, mix_width=mix_width, batch=batch, seq=seq,
                           tile=sb_tile, heads_per_step=8)
        mixed = _sb_sample(sb_q, k_s, v_s, cache_sb_k, cache_sb_v, g_sb_head[l], mixed,
                           layer=l, row_offset=tp, tile=sb_tile)
        c0 = jnp.zeros((batch, ML_HEADS, ML_V_DIM, ML_QK_DIM), F32)
        n0 = jnp.zeros((batch, ML_HEADS, ML_QK_DIM), F32)
        m0 = jnp.zeros((batch, ML_HEADS, V7X_LANES), F32)
        mixed, c_p, n_p, m_p = _mlstm(ml_q, ml_k, ml_v, ml_o, gates, g_ml_head[l], c0, n0, m0, mixed,
                                      batch=batch, seq=seq, chunk=ml_chunk, row_offset=0, col_offset=sb_width)
        m0_s = jnp.broadcast_to(state_ml_m[l][:, :, None], (dec_batch, ML_HEADS, V7X_LANES))
        mixed, c_s, n_s, m_s = _mlstm(ml_q, ml_k, ml_v, ml_o, gates, g_ml_head[l], state_ml_c[l], state_ml_n[l],
                                      m0_s, mixed, batch=dec_batch, seq=dec_seq, chunk=dec_seq,
                                      row_offset=tp, col_offset=sb_width)

        (x,) = _matmul_fullk(mixed, [w_out[l]], [0], d_model, [F32], _ep_residual, extras=(x,),
                             tn=256, name="out_proj", **tall)
        h = _rmsnorm(x, ln_ffn2[l], BF16, rows=t_all, row_tile=norm_tile)
        x = half_ffn(x, h, w_ffn2_gate[l], w_ffn2_up[l], w_ffn2_down[l])

        h = _rmsnorm(x, ln_ple[l], BF16, rows=t_all, row_tile=norm_tile)
        p = jnp.concatenate([p_prompt[l].reshape(tp, -1), p_sample[l].reshape(ts, -1)], axis=0)
        x = _ple(h, w_ple_gate[l], p, w_ple_proj[l], x, tm=tm_tall, tn=256)

        outs["pk"].append(k_p.reshape(batch, seq, sb_heads, SB_HEAD_DIM))
        outs["pv"].append(v_p.reshape(batch, seq, sb_heads, SB_HEAD_DIM))
        outs["pc"].append(c_p)
        outs["pn"].append(n_p)
        outs["pm"].append(m_p[:, :, 0])
        outs["sk"].append(k_s.reshape(dec_batch, dec_seq, sb_heads, SB_HEAD_DIM))
        outs["sv"].append(v_s.reshape(dec_batch, dec_seq, sb_heads, SB_HEAD_DIM))
        outs["sc"].append(c_s)
        outs["sn"].append(n_s)
        outs["sm"].append(m_s[:, :, 0])

    y_prompt = _rmsnorm(x, ln_final, F32, rows=tp, row_tile=256).reshape(batch, seq, d_model)
    y_sample = _rmsnorm(x, ln_final, F32, rows=ts, row_tile=ts, row_offset=tp).reshape(dec_batch, dec_seq, d_model)

    def stack(name):
        vals = outs[name]
        return vals[0][None] if len(vals) == 1 else jnp.stack(vals)

    return (y_prompt, y_sample, stack("pk"), stack("pv"), stack("pc"), stack("pn"), stack("pm"),
            stack("sk"), stack("sv"), stack("sc"), stack("sn"), stack("sm"))
```

```python
import functools
import math

import jax
import jax.numpy as jnp
from jax import lax
from jax.experimental import pallas as pl
from jax.experimental.pallas import tpu as pltpu

F32 = jnp.float32
BF16 = jnp.bfloat16
EPS = 1e-6
LOG2_E = math.log2(math.e)

V7X_VMEM_BYTES = 64 * 1024 * 1024
V7X_LANES = 128
V7X_MXU_DIM = 256
VMEM_REQUEST_CAP = V7X_VMEM_BYTES - 6 * 1024 * 1024

SB_HEAD_DIM = 128
ML_QK_DIM = 256
ML_V_DIM = 512
ML_HEADS = 4
N_GATES = 2 * ML_HEADS
SB_Q_SCALE = LOG2_E * SB_HEAD_DIM ** -0.5
SB_TILES_PER_TRIP = 4


def _vmem_limit(block_bytes, scratch_bytes=0, temp_bytes=0):
    est = 2 * block_bytes + scratch_bytes + temp_bytes + (2 << 20)
    return int(min(max(est, 16 << 20), VMEM_REQUEST_CAP))


def _nbytes(shape, dtype):
    n = 1
    for s in shape:
        n *= s
    return n * jnp.dtype(dtype).itemsize


def _sigmoid(x):
    return 1.0 / (1.0 + jnp.exp(-x))


def _rmsnorm_kernel(x_ref, g_ref, o_ref):
    x = x_ref[...]
    y = x * lax.rsqrt(jnp.mean(x * x, axis=-1, keepdims=True) + EPS)
    o_ref[...] = (y * g_ref[...]).astype(o_ref.dtype)


def _rmsnorm(x, g, out_dtype, *, rows, row_tile, row_offset=0):
    d = x.shape[1]
    off = row_offset // row_tile
    blocks = _nbytes((row_tile, d), F32) + _nbytes((row_tile, d), out_dtype)
    return pl.pallas_call(
        _rmsnorm_kernel,
        out_shape=jax.ShapeDtypeStruct((rows, d), out_dtype),
        grid=(rows // row_tile,),
        in_specs=[pl.BlockSpec((row_tile, d), lambda i: (i + off, 0)),
                  pl.BlockSpec((1, d), lambda i: (0, 0))],
        out_specs=pl.BlockSpec((row_tile, d), lambda i: (i, 0)),
        compiler_params=pltpu.CompilerParams(
            dimension_semantics=("parallel",),
            vmem_limit_bytes=_vmem_limit(blocks, temp_bytes=2 * _nbytes((row_tile, d), F32))),
        name="rmsnorm",
    )(x, g.reshape(1, d))


def _stack_kernel(xp_ref, xs_ref, g_ref, x_ref, h_ref, *, n_prompt_tiles):
    i = pl.program_id(0)

    def emit(src_ref):
        x_ref[...] = src_ref[...]
        _rmsnorm_kernel(src_ref, g_ref, h_ref)

    pl.when(i < n_prompt_tiles)(lambda: emit(xp_ref))
    pl.when(i >= n_prompt_tiles)(lambda: emit(xs_ref))


def _stack_tokens(x_prompt, x_sample, gain, *, tile):
    tp, d = x_prompt.shape
    ts = x_sample.shape[0]
    n_p, n_s = tp // tile, ts // tile
    t_all = tp + ts
    blocks = 3 * _nbytes((tile, d), F32) + _nbytes((tile, d), BF16)
    row = lambda i: (i, 0)
    return pl.pallas_call(
        functools.partial(_stack_kernel, n_prompt_tiles=n_p),
        out_shape=[jax.ShapeDtypeStruct((t_all, d), F32), jax.ShapeDtypeStruct((t_all, d), BF16)],
        grid=(n_p + n_s,),
        in_specs=[pl.BlockSpec((tile, d), lambda i: (jnp.minimum(i, n_p - 1), 0)),
                  pl.BlockSpec((tile, d), lambda i: (jnp.maximum(i - n_p, 0), 0)),
                  pl.BlockSpec((1, d), lambda i: (0, 0))],
        out_specs=[pl.BlockSpec((tile, d), row), pl.BlockSpec((tile, d), row)],
        compiler_params=pltpu.CompilerParams(
            dimension_semantics=("parallel",),
            vmem_limit_bytes=_vmem_limit(blocks, temp_bytes=2 * _nbytes((tile, d), F32))),
        name="stack_tokens",
    )(x_prompt, x_sample, gain.reshape(1, d))


def _mm_kernel(*refs, n_w, n_extra, epilogue, w_transposed):
    lhs_ref = refs[0]
    w_refs = refs[1:1 + n_w]
    extra_refs = refs[1 + n_w:1 + n_w + n_extra]
    out_refs = refs[1 + n_w + n_extra:]
    lhs = lhs_ref[...]
    w_contract = 1 if w_transposed else 0
    accs = [lax.dot_general(lhs, w[...].astype(BF16), (((1,), (w_contract,)), ((), ())),
                            preferred_element_type=F32) for w in w_refs]
    outs = epilogue(accs, [e[...] for e in extra_refs])
    for o_ref, o in zip(out_refs, outs):
        o_ref[...] = o.astype(o_ref.dtype)


def _matmul_fullk(lhs, ws, col_offsets, n_cols, out_dtypes, epilogue, extras=(), *,
                  tm, tn, rows=None, row_offset=0, w_transposed=False, single_buffer_lhs=False, name):
    k_dim = lhs.shape[1]
    rows = lhs.shape[0] if rows is None else rows
    roff = row_offset // tm
    lhs_mode = dict(pipeline_mode=pl.Buffered(1)) if single_buffer_lhs else {}
    in_specs = [pl.BlockSpec((tm, k_dim), lambda i, j: (i + roff, 0), **lhs_mode)]
    for off in col_offsets:
        if w_transposed:
            in_specs.append(pl.BlockSpec((tn, k_dim), lambda i, j, o=off // tn: (j + o, 0)))
        else:
            in_specs.append(pl.BlockSpec((k_dim, tn), lambda i, j, o=off // tn: (0, j + o)))
    extra_bytes = 0
    for e in extras:
        if e.shape[0] == 1:
            in_specs.append(pl.BlockSpec((1, tn), lambda i, j: (0, j)))
            extra_bytes += _nbytes((8, tn), F32)
        else:
            in_specs.append(pl.BlockSpec((tm, tn), lambda i, j: (i + roff, j)))
            extra_bytes += _nbytes((tm, tn), F32)
    out_specs = [pl.BlockSpec((tm, tn), lambda i, j: (i, j)) for _ in out_dtypes]
    out_shape = [jax.ShapeDtypeStruct((rows, n_cols), dt) for dt in out_dtypes]
    out_bytes = sum(_nbytes((tm, tn), dt) for dt in out_dtypes)
    blocks = _nbytes((tm, k_dim), BF16) + len(ws) * _nbytes((k_dim, tn), F32) + extra_bytes + out_bytes
    temps = (len(ws) * ((3 if w_transposed else 1) * _nbytes((k_dim, tn), BF16) + 4 * _nbytes((tm, tn), F32))
             + len(extras) * _nbytes((tm, tn), F32))
    if single_buffer_lhs:
        temps -= _nbytes((tm, k_dim), BF16)
    return pl.pallas_call(
        functools.partial(_mm_kernel, n_w=len(ws), n_extra=len(extras), epilogue=epilogue,
                          w_transposed=w_transposed),
        out_shape=out_shape,
        grid=(rows // tm, n_cols // tn),
        in_specs=in_specs,
        out_specs=out_specs,
        compiler_params=pltpu.CompilerParams(
            dimension_semantics=("parallel", "arbitrary"),
            vmem_limit_bytes=_vmem_limit(blocks, temp_bytes=temps)),
        name=name,
    )(lhs, *ws, *extras)


def _ep_half_swiglu(accs, extras):
    g, u = accs
    return [0.5 * (g * _sigmoid(g) * u)]


def _ep_identity(accs, extras):
    return list(accs)


def _ep_kv(accs, extras):
    k, v = accs
    return [k, v, k, v]


def _ep_residual(accs, extras):
    return [extras[0] + accs[0]]


def _ep_gates(accs, extras):
    lane = lax.broadcasted_iota(jnp.int32, accs[0].shape, 1)
    return [jnp.where(lane < N_GATES, accs[0] + extras[0], 0.0)]


def _ep_sb_q(accs, extras):
    return [accs[0] * SB_Q_SCALE]


def _ep_ml_qk(accs, extras):
    q, k = accs
    return [q * (ML_QK_DIM ** -0.5), k]


def _down_kernel(a_ref, w_ref, x_ref, o_ref, *, nk, k_valid_last, tk):
    k = pl.program_id(2)

    def partial_product(masked):
        a = a_ref[...]
        w = w_ref[...]
        if masked:
            a = jnp.where(lax.broadcasted_iota(jnp.int32, a.shape, 1) < k_valid_last, a, jnp.zeros_like(a))
            w = jnp.where(lax.broadcasted_iota(jnp.int32, w.shape, 0) < k_valid_last, w, jnp.zeros_like(w))
        return jnp.dot(a, w.astype(BF16), preferred_element_type=F32)

    @pl.when(k == 0)
    def _():
        o_ref[...] = x_ref[...] + partial_product(nk == 1 and k_valid_last != tk)

    if nk > 1:
        if k_valid_last == tk:
            @pl.when(k > 0)
            def _():
                o_ref[...] += partial_product(False)
        else:
            @pl.when(jnp.logical_and(k > 0, k < nk - 1))
            def _():
                o_ref[...] += partial_product(False)

            @pl.when(k == nk - 1)
            def _():
                o_ref[...] += partial_product(True)


def _ffn_down(a, w_down, x, *, tm, tn, tk):
    t, f = a.shape
    d = w_down.shape[1]
    nk = pl.cdiv(f, tk)
    k_valid_last = f - (nk - 1) * tk
    blocks = (_nbytes((tm, tk), BF16) + _nbytes((tk, tn), F32) + 2 * _nbytes((tm, tn), F32))
    return pl.pallas_call(
        functools.partial(_down_kernel, nk=nk, k_valid_last=k_valid_last, tk=tk),
        out_shape=jax.ShapeDtypeStruct((t, d), F32),
        grid=(t // tm, d // tn, nk),
        in_specs=[pl.BlockSpec((tm, tk), lambda i, j, k: (i, k)),
                  pl.BlockSpec((tk, tn), lambda i, j, k: (k, j)),
                  pl.BlockSpec((tm, tn), lambda i, j, k: (i, j))],
        out_specs=pl.BlockSpec((tm, tn), lambda i, j, k: (i, j)),
        compiler_params=pltpu.CompilerParams(
            dimension_semantics=("parallel", "parallel", "arbitrary"),
            vmem_limit_bytes=_vmem_limit(blocks, temp_bytes=_nbytes((tk, tn), BF16) + (4 << 20))),
        name="ffn_down",
    )(a, w_down, x)


def _ple_kernel(h_ref, wg_ref, p_ref, wp_ref, x_ref, o_ref):
    gate = _sigmoid(jnp.dot(h_ref[...], wg_ref[...].astype(BF16), preferred_element_type=F32))
    proj = jnp.dot(p_ref[...].astype(BF16), wp_ref[...].astype(BF16), preferred_element_type=F32)
    o_ref[...] = x_ref[...] + gate * proj


def _ple(h, w_gate, p, w_proj, x, *, tm, tn):
    t, d = h.shape
    pd = p.shape[1]
    row_blocks = _nbytes((tm, d), BF16) + _nbytes((tm, pd), F32)
    blocks = _nbytes((d, tn), F32) + _nbytes((pd, tn), F32) + 2 * _nbytes((tm, tn), F32)
    once = dict(pipeline_mode=pl.Buffered(1))
    return pl.pallas_call(
        _ple_kernel,
        out_shape=jax.ShapeDtypeStruct((t, d), F32),
        grid=(t // tm, d // tn),
        in_specs=[pl.BlockSpec((tm, d), lambda i, j: (i, 0), **once),
                  pl.BlockSpec((d, tn), lambda i, j: (0, j)),
                  pl.BlockSpec((tm, pd), lambda i, j: (i, 0), **once),
                  pl.BlockSpec((pd, tn), lambda i, j: (0, j)),
                  pl.BlockSpec((tm, tn), lambda i, j: (i, j))],
        out_specs=pl.BlockSpec((tm, tn), lambda i, j: (i, j)),
        compiler_params=pltpu.CompilerParams(
            dimension_semantics=("parallel", "arbitrary"),
            vmem_limit_bytes=_vmem_limit(blocks, scratch_bytes=row_blocks,
                                         temp_bytes=_nbytes((d, tn), BF16) + 3 * _nbytes((tm, tn), F32))),
        name="ple",
    )(h, w_gate, p, w_proj, x)


def _split3(x):
    hi = x.astype(BF16)
    r = x - hi.astype(F32)
    mid = r.astype(BF16)
    lo = (r - mid.astype(F32)).astype(BF16)
    return hi, mid, lo


def _neg_inclusive_matrix(tk):
    r = lax.broadcasted_iota(jnp.int32, (2 * tk, tk), 0)
    c = lax.broadcasted_iota(jnp.int32, (2 * tk, tk), 1)
    r = jnp.where(r >= tk, r - tk, r)
    return jnp.where(r >= c, -1.0, 0.0).astype(BF16)


def _neg_abs(x):
    bits = lax.bitcast_convert_type(x, jnp.uint32) | jnp.uint32(0x80000000)
    return lax.bitcast_convert_type(bits, F32)


def _sb_scores(q, kblk, *, diagonal):
    z2 = lax.dot_general(q, kblk, (((2,), (2,)), ((0,), (0,))), preferred_element_type=F32)
    heads, tq, tk = z2.shape
    sp2 = jnp.maximum(z2, 0.0) + jnp.log2(1.0 + jnp.exp2(_neg_abs(z2)))
    if diagonal:
        row = lax.broadcasted_iota(jnp.int32, (1, tq, tk), 1)
        col = lax.broadcasted_iota(jnp.int32, (1, tq, tk), 2)
        mask = col < row
        sp2 = jnp.where(mask, sp2, 0.0)
        z2 = jnp.where(mask, z2, -jnp.inf)
    hi = sp2.astype(BF16)
    lo = (sp2 - hi.astype(F32)).astype(BF16)
    parts = jnp.concatenate([hi, lo], axis=-1).reshape(heads * tq, 2 * tk)
    return z2, parts


def _sb_accumulate(z2, parts, vblk, neg_incl, run, acc):
    heads, tq, tk = z2.shape
    incl = jnp.dot(parts, neg_incl, preferred_element_type=F32).reshape(heads, tq, tk)
    a = jnp.exp2((z2 + run) + incl)
    acc = acc + lax.dot_general(a.astype(BF16), vblk, (((2,), (1,)), ((0,), (0,))), preferred_element_type=F32)
    run = run + incl[:, :, 0:1]
    return run, acc


def _head_rmsnorm(x, g):
    return x * lax.rsqrt(jnp.mean(x * x, axis=-1, keepdims=True) + EPS) * g


def _sb_prompt_kernel(q_ref, k_ref, v_ref, g_ref, mixed_ref, o_ref, *, tile, heads, tiles_per_trip):
    del mixed_ref
    qi = pl.program_id(2)
    d = SB_HEAD_DIM
    neg_incl = _neg_inclusive_matrix(tile)
    q = jnp.stack([q_ref[:, h * d:(h + 1) * d] for h in range(heads)])

    def key_rows(kb):
        return pl.ds(pl.multiple_of(kb * tile, tile), tile)

    def visit(kb, carry, diagonal):
        kblk = jnp.stack([k_ref[key_rows(kb), h * d:(h + 1) * d] for h in range(heads)])
        vblk = jnp.stack([v_ref[key_rows(kb), h * d:(h + 1) * d] for h in range(heads)])
        z2, parts = _sb_scores(q, kblk, diagonal=diagonal)
        return _sb_accumulate(z2, parts, vblk, neg_incl, *carry)

    def visit_group(kb, carry):
        for g in range(tiles_per_trip):
            carry = visit(kb - g, carry, False)
        return carry

    carry = (jnp.zeros((heads, tile, 1), F32), jnp.zeros((heads, tile, d), F32))
    carry = visit(qi, carry, True)
    singles = qi % tiles_per_trip
    carry = lax.fori_loop(0, singles, lambda t, c: visit(qi - 1 - t, c, False), carry)
    first = qi - 1 - singles
    carry = lax.fori_loop(0, qi // tiles_per_trip, lambda t, c: visit_group(first - tiles_per_trip * t, c), carry)
    for h in range(heads):
        cols = slice(h * d, (h + 1) * d)
        o_ref[:, cols] = _head_rmsnorm(carry[1][h], g_ref[:, cols]).astype(o_ref.dtype)


def _sb_prompt(q, k, v, g_heads, mixed, *, batch, seq, tile, heads_per_step):
    d = SB_HEAD_DIM
    w = heads_per_step * d
    groups = k.shape[1] // w
    nq = seq // tile
    blocks = (2 * _nbytes((tile, w), BF16) + 2 * _nbytes((seq, w), BF16) + _nbytes((8, w), F32))
    return pl.pallas_call(
        functools.partial(_sb_prompt_kernel, tile=tile, heads=heads_per_step, tiles_per_trip=SB_TILES_PER_TRIP),
        out_shape=jax.ShapeDtypeStruct(mixed.shape, mixed.dtype),
        grid=(batch, groups, nq),
        in_specs=[pl.BlockSpec((tile, w), lambda b, h, i: (b * nq + i, h)),
                  pl.BlockSpec((seq, w), lambda b, h, i: (b, h)),
                  pl.BlockSpec((seq, w), lambda b, h, i: (b, h)),
                  pl.BlockSpec((None, 1, w), lambda b, h, i: (h, 0, 0)),
                  pl.BlockSpec(memory_space=pl.ANY)],
        out_specs=pl.BlockSpec((tile, w), lambda b, h, i: (b * nq + i, h)),
        input_output_aliases={4: 0},
        compiler_params=pltpu.CompilerParams(
            dimension_semantics=("parallel", "parallel", "arbitrary"),
            vmem_limit_bytes=_vmem_limit(blocks, temp_bytes=16 * heads_per_step * _nbytes((tile, tile), F32))),
        name="sb_prompt",
    )(q, k, v, g_heads.reshape(groups, 1, w), mixed)


def _sb_sample_kernel(q_ref, kn_ref, vn_ref, kc_ref, vc_ref, g_ref, mixed_ref, o_ref, run_ref, acc_ref,
                      *, heads, tile):
    del mixed_ref
    kc = pl.program_id(1)
    nkc = pl.num_programs(1)
    d = SB_HEAD_DIM
    n_new = kn_ref.shape[0]
    q = jnp.stack([q_ref[:, h * d:(h + 1) * d] for h in range(heads)])

    @pl.when(kc == 0)
    def _():
        pad = jnp.zeros((heads, V7X_LANES - n_new, d), BF16)
        kblk = jnp.concatenate([jnp.stack([kn_ref[:, h * d:(h + 1) * d] for h in range(heads)]).astype(BF16), pad], axis=1)
        vblk = jnp.concatenate([jnp.stack([vn_ref[:, h * d:(h + 1) * d] for h in range(heads)]).astype(BF16), pad], axis=1)
        z2, parts = _sb_scores(q, kblk, diagonal=True)
        run, acc = _sb_accumulate(z2, parts, vblk, _neg_inclusive_matrix(V7X_LANES),
                                  jnp.zeros((heads, n_new, 1), F32), jnp.zeros((heads, n_new, d), F32))
        run_ref[...] = run
        acc_ref[...] = acc

    kblk = pltpu.einshape("phd->hpd", kc_ref[...].astype(BF16).reshape(tile, heads, d))
    vblk = pltpu.einshape("phd->hpd", vc_ref[...].astype(BF16).reshape(tile, heads, d))
    z2, parts = _sb_scores(q, kblk, diagonal=False)
    run, acc = _sb_accumulate(z2, parts, vblk, _neg_inclusive_matrix(tile), run_ref[...], acc_ref[...])
    run_ref[...] = run
    acc_ref[...] = acc

    @pl.when(kc == nkc - 1)
    def _():
        for h in range(heads):
            cols = slice(h * d, (h + 1) * d)
            o_ref[:, cols] = _head_rmsnorm(acc_ref[h], g_ref[h:h + 1, :]).astype(o_ref.dtype)


def _sb_sample(q, k_new, v_new, k_cache, v_cache, g_heads, mixed, *, layer, row_offset, tile):
    _, batch, past, heads, d = k_cache.shape
    width = heads * d
    n_new = k_new.shape[0] // batch
    roff = row_offset // n_new
    nkc = past // tile
    blocks = (2 * _nbytes((n_new, width), BF16) + 2 * _nbytes((n_new, width), F32)
              + 2 * _nbytes((tile, heads, d), F32) + _nbytes((heads, d), F32))
    state_bytes = _nbytes((heads, n_new, V7X_LANES), F32) + _nbytes((heads, n_new, d), F32)
    k_cache = k_cache.reshape(-1, batch, past * heads, d)
    v_cache = v_cache.reshape(-1, batch, past * heads, d)
    cache_spec = pl.BlockSpec((None, None, tile * heads, d), lambda b, c: (layer, b, nkc - 1 - c, 0))
    return pl.pallas_call(
        functools.partial(_sb_sample_kernel, heads=heads, tile=tile),
        out_shape=jax.ShapeDtypeStruct(mixed.shape, mixed.dtype),
        grid=(batch, nkc),
        in_specs=[pl.BlockSpec((n_new, width), lambda b, c: (b + roff, 0)),
                  pl.BlockSpec((n_new, width), lambda b, c: (b, 0)),
                  pl.BlockSpec((n_new, width), lambda b, c: (b, 0)),
                  cache_spec,
                  cache_spec,
                  pl.BlockSpec((heads, d), lambda b, c: (0, 0)),
                  pl.BlockSpec(memory_space=pl.ANY)],
        out_specs=pl.BlockSpec((n_new, width), lambda b, c: (b + roff, 0)),
        scratch_shapes=[pltpu.VMEM((heads, n_new, 1), F32), pltpu.VMEM((heads, n_new, d), F32)],
        input_output_aliases={6: 0},
        compiler_params=pltpu.CompilerParams(
            dimension_semantics=("parallel", "arbitrary"),
            vmem_limit_bytes=_vmem_limit(blocks, scratch_bytes=state_bytes, temp_bytes=12 << 20)),
        name="sb_sample",
    )(q, k_new, v_new, k_cache, v_cache, g_heads, mixed)


def _log_sigmoid(x):
    return jnp.minimum(x, 0.0) - jnp.log1p(jnp.exp(-jnp.abs(x)))


def _gate_rows(x, chunk):
    pad = (-chunk) % V7X_LANES
    if pad:
        x = jnp.concatenate([x, jnp.zeros((pad, V7X_LANES), x.dtype)], axis=0)
    return x.T[:N_GATES, :chunk]


def _mlstm_kernel(q_ref, k_ref, v_ref, o_ref, g_ref, gain_ref, c0_ref, n0_ref, m0_ref, mixed_ref,
                  out_ref, c_out_ref, n_out_ref, m_out_ref, c_sc, n_sc, m_sc, *, chunk):
    del mixed_ref
    ci = pl.program_id(1)
    nci = pl.num_programs(1)

    @pl.when(ci == 0)
    def _():
        c_sc[...] = c0_ref[...]
        n_sc[...] = n0_ref[...]
        m_sc[...] = m0_ref[...]

    row = lax.broadcasted_iota(jnp.int32, (chunk, chunk), 0)
    col = lax.broadcasted_iota(jnp.int32, (chunk, chunk), 1)
    causal = col <= row
    tri = causal.astype(BF16)

    g = g_ref[...]
    b_cols = sum(jnp.dot(tri, part, preferred_element_type=F32) for part in _split3(_log_sigmoid(g)))
    g_rows = _gate_rows(g, chunk)
    b_rows = _gate_rows(b_cols, chunk)

    for h in range(ML_HEADS):
        qk = slice(h * ML_QK_DIM, (h + 1) * ML_QK_DIM)
        vv = slice(h * ML_V_DIM, (h + 1) * ML_V_DIM)
        q = q_ref[:, qk]
        k = k_ref[:, qk]
        v = v_ref[:, vv]
        c_prev = c_sc[h]
        n_prev = n_sc[h:h + 1, :]
        m_prev = m_sc[h:h + 1, 0:1]

        b_col = b_cols[:, ML_HEADS + h:ML_HEADS + h + 1]
        i_col = g[:, h:h + 1]
        b_row = b_rows[ML_HEADS + h:ML_HEADS + h + 1, :]
        i_row = g_rows[h:h + 1, :]

        d_log = jnp.where(causal, b_col - b_row + i_row, -jnp.inf)
        inter_log = b_col + m_prev
        m_row = jnp.maximum(inter_log, jnp.max(d_log, axis=-1, keepdims=True))
        w_intra = jnp.exp(d_log - m_row)
        w_inter = jnp.exp(inter_log - m_row)

        s = lax.dot_general(q, k, (((1,), (1,)), ((), ())), preferred_element_type=F32) * w_intra
        inter = lax.dot_general(q, c_prev.astype(BF16), (((1,), (1,)), ((), ())), preferred_element_type=F32)
        num = w_inter * inter + jnp.dot(s.astype(BF16), v, preferred_element_type=F32)
        den = w_inter * jnp.sum(q.astype(F32) * n_prev, axis=-1, keepdims=True) + jnp.sum(s, axis=-1, keepdims=True)
        hid = num / jnp.maximum(jnp.abs(den), jnp.exp(-m_row))

        m_new = m_row[chunk - 1:chunk, :]
        b_last = b_col[chunk - 1:chunk, :]
        w_prev = jnp.exp(b_last + m_prev - m_new)
        w_tok = jnp.exp(b_last - b_col + i_col - m_new)
        vw = (v.astype(F32) * w_tok).astype(BF16)
        c_sc[h] = w_prev * c_prev + lax.dot_general(vw, k, (((0,), (0,)), ((), ())), preferred_element_type=F32)
        n_sc[h:h + 1, :] = w_prev * n_prev + jnp.sum(k.astype(F32) * w_tok, axis=0, keepdims=True)
        m_sc[h:h + 1, :] = jnp.broadcast_to(m_new, (1, m_sc.shape[1]))

        normed = _head_rmsnorm(hid, gain_ref[h:h + 1, :])
        out_ref[:, vv] = (normed * _sigmoid(o_ref[:, vv])).astype(out_ref.dtype)

    @pl.when(ci == nci - 1)
    def _():
        c_out_ref[...] = c_sc[...]
        n_out_ref[...] = n_sc[...]
        m_out_ref[...] = m_sc[...]


def _mlstm(q, k, v, o, g, gain, c0, n0, m0, mixed, *, batch, seq, chunk, row_offset, col_offset):
    nci = seq // chunk
    roff = row_offset // chunk
    vw = ML_HEADS * ML_V_DIM
    qw = ML_HEADS * ML_QK_DIM
    coff = col_offset // vw
    state_bytes = (_nbytes((ML_HEADS, ML_V_DIM, ML_QK_DIM), F32) + _nbytes((ML_HEADS, ML_QK_DIM), F32)
                   + _nbytes((ML_HEADS, V7X_LANES), F32))
    blocks = (2 * _nbytes((chunk, qw), BF16) + 2 * _nbytes((chunk, vw), BF16) + _nbytes((chunk, vw), F32)
              + _nbytes((chunk, V7X_LANES), F32) + 2 * state_bytes)
    row_map = lambda b, c: (b * nci + c + roff, 0)
    state_specs = [pl.BlockSpec((None, ML_HEADS, ML_V_DIM, ML_QK_DIM), lambda b, c: (b, 0, 0, 0)),
                   pl.BlockSpec((None, ML_HEADS, ML_QK_DIM), lambda b, c: (b, 0, 0)),
                   pl.BlockSpec((None, ML_HEADS, V7X_LANES), lambda b, c: (b, 0, 0))]
    return pl.pallas_call(
        functools.partial(_mlstm_kernel, chunk=chunk),
        out_shape=[jax.ShapeDtypeStruct(mixed.shape, mixed.dtype),
                   jax.ShapeDtypeStruct(c0.shape, F32),
                   jax.ShapeDtypeStruct(n0.shape, F32),
                   jax.ShapeDtypeStruct(m0.shape, F32)],
        grid=(batch, nci),
        in_specs=[pl.BlockSpec((chunk, qw), row_map),
                  pl.BlockSpec((chunk, qw), row_map),
                  pl.BlockSpec((chunk, vw), row_map),
                  pl.BlockSpec((chunk, vw), row_map),
                  pl.BlockSpec((chunk, V7X_LANES), row_map),
                  pl.BlockSpec((ML_HEADS, ML_V_DIM), lambda b, c: (0, 0)),
                  *state_specs,
                  pl.BlockSpec(memory_space=pl.ANY)],
        out_specs=[pl.BlockSpec((chunk, vw), lambda b, c: (b * nci + c + roff, coff)), *state_specs],
        scratch_shapes=[pltpu.VMEM((ML_HEADS, ML_V_DIM, ML_QK_DIM), F32),
                        pltpu.VMEM((ML_HEADS, ML_QK_DIM), F32),
                        pltpu.VMEM((ML_HEADS, V7X_LANES), F32)],
        input_output_aliases={9: 0},
        compiler_params=pltpu.CompilerParams(
            dimension_semantics=("parallel", "arbitrary"),
            vmem_limit_bytes=_vmem_limit(blocks, scratch_bytes=state_bytes, temp_bytes=12 << 20)),
        name="mlstm",
    )(q, k, v, o, g, gain, c0, n0, m0, mixed)


def _pick_row_tile(t, target):
    best = 16
    for cand in range(16, target + 1, 16):
        if t % cand == 0:
            best = cand
    return best


def kernel(x_prompt, x_sample, cache_sb_k, cache_sb_v, state_ml_c, state_ml_n, state_ml_m, p_prompt, p_sample, ln_ffn1, w_ffn1_gate, w_ffn1_up, w_ffn1_down, ln_mix, w_in, b_if, g_sb_head, g_ml_head, w_out, ln_ffn2, w_ffn2_gate, w_ffn2_up, w_ffn2_down, ln_ple, w_ple_gate, w_ple_proj, ln_final):
    depth = w_in.shape[0]
    batch, seq, d_model = x_prompt.shape
    dec_batch, dec_seq, _ = x_sample.shape
    tp = batch * seq
    ts = dec_batch * dec_seq
    t_all = tp + ts
    sb_heads = g_sb_head.shape[1]
    sb_width = sb_heads * SB_HEAD_DIM
    ml_qk_width = ML_HEADS * ML_QK_DIM
    ml_v_width = ML_HEADS * ML_V_DIM
    mix_width = sb_width + ml_v_width
    d_ff = w_ffn1_gate.shape[2]
    off_q, off_k, off_v = 0, sb_width, 2 * sb_width
    off_mq = 3 * sb_width
    off_mk = off_mq + ml_qk_width
    off_mv = off_mk + ml_qk_width
    off_mo = off_mv + ml_v_width
    off_g = off_mo + ml_v_width

    tm_tall = _pick_row_tile(t_all, 2176)
    tm_prompt = _pick_row_tile(tp, 2176)
    norm_tile = _pick_row_tile(t_all, 384)
    ff_tn = V7X_MXU_DIM
    sb_tile = V7X_MXU_DIM
    ml_chunk = min(seq, V7X_MXU_DIM)
    tall = dict(tm=tm_tall, single_buffer_lhs=True)

    x, h = _stack_tokens(x_prompt.reshape(tp, d_model), x_sample.reshape(ts, d_model), ln_ffn1[0],
                         tile=math.gcd(tp, ts))

    def half_ffn(x, h, w_gate, w_up, w_down):
        a = _matmul_fullk(h, [w_gate, w_up], [0, 0], d_ff, [BF16], _ep_half_swiglu, tn=ff_tn,
                          name="ffn_gate_up", **tall)[0]
        return _ffn_down(a, w_down, x, tm=tm_tall, tn=1024, tk=1024)

    outs = {name: [] for name in ("pk", "pv", "pc", "pn", "pm", "sk", "sv", "sc", "sn", "sm")}
    for l in range(depth):
        if l > 0:
            h = _rmsnorm(x, ln_ffn1[l], BF16, rows=t_all, row_tile=norm_tile)
        x = half_ffn(x, h, w_ffn1_gate[l], w_ffn1_up[l], w_ffn1_down[l])

        h = _rmsnorm(x, ln_mix[l], BF16, rows=t_all, row_tile=norm_tile)
        w = jnp.swapaxes(w_in[l], 0, 1)
        proj = functools.partial(_matmul_fullk, h, w_transposed=True)
        (sb_q,) = proj([w], [off_q], sb_width, [BF16], _ep_sb_q, tm=tm_tall, tn=256, name="proj_sb_q")
        k_p, v_p, k_pb, v_pb = proj([w, w], [off_k, off_v], sb_width, [F32, F32, BF16, BF16], _ep_kv,
                                    tm=tm_prompt, single_buffer_lhs=True, tn=256, rows=tp, name="proj_sb_kv_prompt")
        k_s, v_s = proj([w, w], [off_k, off_v], sb_width, [F32, F32], _ep_identity,
                        tm=ts, tn=256, rows=ts, row_offset=tp, name="proj_sb_kv_sample")
        ml_q, ml_k = proj([w, w], [off_mq, off_mk], ml_qk_width, [BF16, BF16], _ep_ml_qk,
                          tn=256, name="proj_ml_qk", **tall)
        ml_v, ml_o = proj([w, w], [off_mv, off_mo], ml_v_width, [BF16, F32], _ep_identity,
                          tn=256, name="proj_ml_vo", **tall)
        b_row = jnp.pad(b_if[l], (0, V7X_LANES - N_GATES)).reshape(1, V7X_LANES)
        (gates,) = proj([w], [off_g], V7X_LANES, [F32], _ep_gates, extras=(b_row,),
                        tm=tm_tall, tn=V7X_LANES, name="proj_ml_gates")

        assert h.shape == (t_all, mix_width) and h.dtype == BF16
        mixed = _sb_prompt(sb_q, k_pb, v_pb, g_sb_head[l], h, batch=batch, seq=seq,
                           tile=sb_tile, heads_per_step=8)
        mixed = _sb_sample(sb_q, k_s, v_s, cache_sb_k, cache_sb_v, g_sb_head[l], mixed,
                           layer=l, row_offset=tp, tile=sb_tile)
        c0 = jnp.zeros((batch, ML_HEADS, ML_V_DIM, ML_QK_DIM), F32)
        n0 = jnp.zeros((batch, ML_HEADS, ML_QK_DIM), F32)
        m0 = jnp.zeros((batch, ML_HEADS, V7X_LANES), F32)
        mixed, c_p, n_p, m_p = _mlstm(ml_q, ml_k, ml_v, ml_o, gates, g_ml_head[l], c0, n0, m0, mixed,
                                      batch=batch, seq=seq, chunk=ml_chunk, row_offset=0, col_offset=sb_width)
        m0_s = jnp.broadcast_to(state_ml_m[l][:, :, None], (dec_batch, ML_HEADS, V7X_LANES))
        mixed, c_s, n_s, m_s = _mlstm(ml_q, ml_k, ml_v, ml_o, gates, g_ml_head[l], state_ml_c[l], state_ml_n[l],
                                      m0_s, mixed, batch=dec_batch, seq=dec_seq, chunk=dec_seq,
                                      row_offset=tp, col_offset=sb_width)

        (x,) = _matmul_fullk(mixed, [w_out[l]], [0], d_model, [F32], _ep_residual, extras=(x,),
                             tn=256, name="out_proj", **tall)
        h = _rmsnorm(x, ln_ffn2[l], BF16, rows=t_all, row_tile=norm_tile)
        x = half_ffn(x, h, w_ffn2_gate[l], w_ffn2_up[l], w_ffn2_down[l])

        h = _rmsnorm(x, ln_ple[l], BF16, rows=t_all, row_tile=norm_tile)
        p = jnp.concatenate([p_prompt[l].reshape(tp, -1), p_sample[l].reshape(ts, -1)], axis=0)
        x = _ple(h, w_ple_gate[l], p, w_ple_proj[l], x, tm=tm_tall, tn=256)

        outs["pk"].append(k_p.reshape(batch, seq, sb_heads, SB_HEAD_DIM))
        outs["pv"].append(v_p.reshape(batch, seq, sb_heads, SB_HEAD_DIM))
        outs["pc"].append(c_p)
        outs["pn"].append(n_p)
        outs["pm"].append(m_p[:, :, 0])
        outs["sk"].append(k_s.reshape(dec_batch, dec_seq, sb_heads, SB_HEAD_DIM))
        outs["sv"].append(v_s.reshape(dec_batch, dec_seq, sb_heads, SB_HEAD_DIM))
        outs["sc"].append(c_s)
        outs["sn"].append(n_s)
        outs["sm"].append(m_s[:, :, 0])

    y_prompt = _rmsnorm(x, ln_final, F32, rows=tp, row_tile=256).reshape(batch, seq, d_model)
    y_sample = _rmsnorm(x, ln_final, F32, rows=ts, row_tile=ts, row_offset=tp).reshape(dec_batch, dec_seq, d_model)

    def stack(name):
        vals = outs[name]
        return vals[0][None] if len(vals) == 1 else jnp.stack(vals)

    return (y_prompt, y_sample, stack("pk"), stack("pv"), stack("pc"), stack("pn"), stack("pm"),
            stack("sk"), stack("sv"), stack("sc"), stack("sn"), stack("sm"))
```

```python
import functools
import math

import jax
import jax.numpy as jnp
from jax import lax
from jax.experimental import pallas as pl
from jax.experimental.pallas import tpu as pltpu

F32 = jnp.float32
BF16 = jnp.bfloat16
EPS = 1e-6
LOG2_E = math.log2(math.e)

V7X_VMEM_BYTES = 64 * 1024 * 1024
V7X_LANES = 128
V7X_MXU_DIM = 256
VMEM_REQUEST_CAP = V7X_VMEM_BYTES - 6 * 1024 * 1024

SB_HEAD_DIM = 128
ML_QK_DIM = 256
ML_V_DIM = 512
ML_HEADS = 4
N_GATES = 2 * ML_HEADS
SB_Q_SCALE = LOG2_E * SB_HEAD_DIM ** -0.5
SB_TILES_PER_TRIP = 4
SB_HEADS_PER_STEP = 8

MATMUL_ROW_TILE_TARGET = 2176
NORM_ROW_TILE_TARGET = 384
FINAL_NORM_ROW_TILE = 256
DOWN_COL_TILE = 1024
DOWN_K_TILE = 1024


def _vmem_limit(block_bytes, scratch_bytes=0, temp_bytes=0):
    est = 2 * block_bytes + scratch_bytes + temp_bytes + (2 << 20)
    return int(min(max(est, 16 << 20), VMEM_REQUEST_CAP))


def _nbytes(shape, dtype):
    n = 1
    for s in shape:
        n *= s
    return n * jnp.dtype(dtype).itemsize


def _sigmoid(x):
    return 1.0 / (1.0 + jnp.exp(-x))


def _rmsnorm_kernel(x_ref, g_ref, o_ref):
    x = x_ref[...]
    y = x * lax.rsqrt(jnp.mean(x * x, axis=-1, keepdims=True) + EPS)
    o_ref[...] = (y * g_ref[...]).astype(o_ref.dtype)


def _rmsnorm(x, g, out_dtype, *, rows, row_tile, row_offset=0):
    d = x.shape[1]
    off = row_offset // row_tile
    blocks = _nbytes((row_tile, d), F32) + _nbytes((row_tile, d), out_dtype)
    return pl.pallas_call(
        _rmsnorm_kernel,
        out_shape=jax.ShapeDtypeStruct((rows, d), out_dtype),
        grid=(rows // row_tile,),
        in_specs=[pl.BlockSpec((row_tile, d), lambda i: (i + off, 0)),
                  pl.BlockSpec((1, d), lambda i: (0, 0))],
        out_specs=pl.BlockSpec((row_tile, d), lambda i: (i, 0)),
        compiler_params=pltpu.CompilerParams(
            dimension_semantics=("parallel",),
            vmem_limit_bytes=_vmem_limit(blocks, temp_bytes=2 * _nbytes((row_tile, d), F32))),
        name="rmsnorm",
    )(x, g.reshape(1, d))


def _stack_kernel(xp_ref, xs_ref, g_ref, x_ref, h_ref, *, n_prompt_tiles):
    i = pl.program_id(0)

    def emit(src_ref):
        x_ref[...] = src_ref[...]
        _rmsnorm_kernel(src_ref, g_ref, h_ref)

    pl.when(i < n_prompt_tiles)(lambda: emit(xp_ref))
    pl.when(i >= n_prompt_tiles)(lambda: emit(xs_ref))


def _stack_tokens(x_prompt, x_sample, gain, *, tile):
    tp, d = x_prompt.shape
    ts = x_sample.shape[0]
    n_p, n_s = tp // tile, ts // tile
    t_all = tp + ts
    blocks = 3 * _nbytes((tile, d), F32) + _nbytes((tile, d), BF16)
    row = lambda i: (i, 0)
    return pl.pallas_call(
        functools.partial(_stack_kernel, n_prompt_tiles=n_p),
        out_shape=[jax.ShapeDtypeStruct((t_all, d), F32), jax.ShapeDtypeStruct((t_all, d), BF16)],
        grid=(n_p + n_s,),
        in_specs=[pl.BlockSpec((tile, d), lambda i: (jnp.minimum(i, n_p - 1), 0)),
                  pl.BlockSpec((tile, d), lambda i: (jnp.maximum(i - n_p, 0), 0)),
                  pl.BlockSpec((1, d), lambda i: (0, 0))],
        out_specs=[pl.BlockSpec((tile, d), row), pl.BlockSpec((tile, d), row)],
        compiler_params=pltpu.CompilerParams(
            dimension_semantics=("parallel",),
            vmem_limit_bytes=_vmem_limit(blocks, temp_bytes=2 * _nbytes((tile, d), F32))),
        name="stack_tokens",
    )(x_prompt, x_sample, gain.reshape(1, d))


def _mm_kernel(*refs, n_w, n_extra, epilogue, w_transposed):
    lhs_ref = refs[0]
    w_refs = refs[1:1 + n_w]
    extra_refs = refs[1 + n_w:1 + n_w + n_extra]
    out_refs = refs[1 + n_w + n_extra:]
    lhs = lhs_ref[...]
    w_contract = 1 if w_transposed else 0
    accs = [lax.dot_general(lhs, w[...].astype(BF16), (((1,), (w_contract,)), ((), ())),
                            preferred_element_type=F32) for w in w_refs]
    outs = epilogue(accs, [e[...] for e in extra_refs])
    for o_ref, o in zip(out_refs, outs):
        o_ref[...] = o.astype(o_ref.dtype)


def _matmul_fullk(lhs, ws, col_offsets, n_cols, out_dtypes, epilogue, extras=(), *,
                  tm, tn, rows=None, row_offset=0, w_transposed=False, single_buffer_lhs=False, name):
    k_dim = lhs.shape[1]
    rows = lhs.shape[0] if rows is None else rows
    roff = row_offset // tm
    lhs_mode = dict(pipeline_mode=pl.Buffered(1)) if single_buffer_lhs else {}
    in_specs = [pl.BlockSpec((tm, k_dim), lambda i, j: (i + roff, 0), **lhs_mode)]
    for off in col_offsets:
        if w_transposed:
            in_specs.append(pl.BlockSpec((tn, k_dim), lambda i, j, o=off // tn: (j + o, 0)))
        else:
            in_specs.append(pl.BlockSpec((k_dim, tn), lambda i, j, o=off // tn: (0, j + o)))
    extra_bytes = 0
    for e in extras:
        if e.shape[0] == 1:
            in_specs.append(pl.BlockSpec((1, tn), lambda i, j: (0, j)))
            extra_bytes += _nbytes((8, tn), F32)
        else:
            in_specs.append(pl.BlockSpec((tm, tn), lambda i, j: (i + roff, j)))
            extra_bytes += _nbytes((tm, tn), F32)
    out_specs = [pl.BlockSpec((tm, tn), lambda i, j: (i, j)) for _ in out_dtypes]
    out_shape = [jax.ShapeDtypeStruct((rows, n_cols), dt) for dt in out_dtypes]
    out_bytes = sum(_nbytes((tm, tn), dt) for dt in out_dtypes)
    blocks = _nbytes((tm, k_dim), BF16) + len(ws) * _nbytes((k_dim, tn), F32) + extra_bytes + out_bytes
    temps = (len(ws) * ((3 if w_transposed else 1) * _nbytes((k_dim, tn), BF16) + 4 * _nbytes((tm, tn), F32))
             + len(extras) * _nbytes((tm, tn), F32))
    if single_buffer_lhs:
        temps -= _nbytes((tm, k_dim), BF16)
    return pl.pallas_call(
        functools.partial(_mm_kernel, n_w=len(ws), n_extra=len(extras), epilogue=epilogue,
                          w_transposed=w_transposed),
        out_shape=out_shape,
        grid=(rows // tm, n_cols // tn),
        in_specs=in_specs,
        out_specs=out_specs,
        compiler_params=pltpu.CompilerParams(
            dimension_semantics=("parallel", "arbitrary"),
            vmem_limit_bytes=_vmem_limit(blocks, temp_bytes=temps)),
        name=name,
    )(lhs, *ws, *extras)


def _ep_half_swiglu(accs, extras):
    g, u = accs
    return [0.5 * (g * _sigmoid(g) * u)]


def _ep_identity(accs, extras):
    return list(accs)


def _ep_kv(accs, extras):
    k, v = accs
    return [k, v, k, v]


def _ep_residual(accs, extras):
    return [extras[0] + accs[0]]


def _ep_gates(accs, extras):
    lane = lax.broadcasted_iota(jnp.int32, accs[0].shape, 1)
    return [jnp.where(lane < N_GATES, accs[0] + extras[0], 0.0)]


def _ep_sb_q(accs, extras):
    return [accs[0] * SB_Q_SCALE]


def _ep_ml_qk(accs, extras):
    q, k = accs
    return [q * (ML_QK_DIM ** -0.5), k]


def _down_kernel(a_ref, w_ref, x_ref, o_ref, *, nk, k_valid_last, tk):
    k = pl.program_id(2)

    def partial_product(masked):
        a = a_ref[...]
        w = w_ref[...]
        if masked:
            a = jnp.where(lax.broadcasted_iota(jnp.int32, a.shape, 1) < k_valid_last, a, jnp.zeros_like(a))
            w = jnp.where(lax.broadcasted_iota(jnp.int32, w.shape, 0) < k_valid_last, w, jnp.zeros_like(w))
        return jnp.dot(a, w.astype(BF16), preferred_element_type=F32)

    @pl.when(k == 0)
    def _():
        o_ref[...] = x_ref[...] + partial_product(nk == 1 and k_valid_last != tk)

    if nk > 1:
        if k_valid_last == tk:
            @pl.when(k > 0)
            def _():
                o_ref[...] += partial_product(False)
        else:
            @pl.when(jnp.logical_and(k > 0, k < nk - 1))
            def _():
                o_ref[...] += partial_product(False)

            @pl.when(k == nk - 1)
            def _():
                o_ref[...] += partial_product(True)


def _ffn_down(a, w_down, x, *, tm, tn, tk):
    t, f = a.shape
    d = w_down.shape[1]
    nk = pl.cdiv(f, tk)
    k_valid_last = f - (nk - 1) * tk
    blocks = (_nbytes((tm, tk), BF16) + _nbytes((tk, tn), F32) + 2 * _nbytes((tm, tn), F32))
    return pl.pallas_call(
        functools.partial(_down_kernel, nk=nk, k_valid_last=k_valid_last, tk=tk),
        out_shape=jax.ShapeDtypeStruct((t, d), F32),
        grid=(t // tm, d // tn, nk),
        in_specs=[pl.BlockSpec((tm, tk), lambda i, j, k: (i, k)),
                  pl.BlockSpec((tk, tn), lambda i, j, k: (k, j)),
                  pl.BlockSpec((tm, tn), lambda i, j, k: (i, j))],
        out_specs=pl.BlockSpec((tm, tn), lambda i, j, k: (i, j)),
        compiler_params=pltpu.CompilerParams(
            dimension_semantics=("parallel", "parallel", "arbitrary"),
            vmem_limit_bytes=_vmem_limit(blocks, temp_bytes=_nbytes((tk, tn), BF16) + (4 << 20))),
        name="ffn_down",
    )(a, w_down, x)


def _ple_kernel(h_ref, wg_ref, p_ref, wp_ref, x_ref, o_ref):
    gate = _sigmoid(jnp.dot(h_ref[...], wg_ref[...].astype(BF16), preferred_element_type=F32))
    proj = jnp.dot(p_ref[...].astype(BF16), wp_ref[...].astype(BF16), preferred_element_type=F32)
    o_ref[...] = x_ref[...] + gate * proj


def _ple(h, w_gate, p, w_proj, x, *, tm, tn):
    t, d = h.shape
    pd = p.shape[1]
    row_blocks = _nbytes((tm, d), BF16) + _nbytes((tm, pd), F32)
    blocks = _nbytes((d, tn), F32) + _nbytes((pd, tn), F32) + 2 * _nbytes((tm, tn), F32)
    once = dict(pipeline_mode=pl.Buffered(1))
    return pl.pallas_call(
        _ple_kernel,
        out_shape=jax.ShapeDtypeStruct((t, d), F32),
        grid=(t // tm, d // tn),
        in_specs=[pl.BlockSpec((tm, d), lambda i, j: (i, 0), **once),
                  pl.BlockSpec((d, tn), lambda i, j: (0, j)),
                  pl.BlockSpec((tm, pd), lambda i, j: (i, 0), **once),
                  pl.BlockSpec((pd, tn), lambda i, j: (0, j)),
                  pl.BlockSpec((tm, tn), lambda i, j: (i, j))],
        out_specs=pl.BlockSpec((tm, tn), lambda i, j: (i, j)),
        compiler_params=pltpu.CompilerParams(
            dimension_semantics=("parallel", "arbitrary"),
            vmem_limit_bytes=_vmem_limit(blocks, scratch_bytes=row_blocks,
                                         temp_bytes=_nbytes((d, tn), BF16) + 3 * _nbytes((tm, tn), F32))),
        name="ple",
    )(h, w_gate, p, w_proj, x)


def _split3(x):
    hi = x.astype(BF16)
    r = x - hi.astype(F32)
    mid = r.astype(BF16)
    lo = (r - mid.astype(F32)).astype(BF16)
    return hi, mid, lo


def _neg_inclusive_matrix(tk):
    r = lax.broadcasted_iota(jnp.int32, (2 * tk, tk), 0)
    c = lax.broadcasted_iota(jnp.int32, (2 * tk, tk), 1)
    r = jnp.where(r >= tk, r - tk, r)
    return jnp.where(r >= c, -1.0, 0.0).astype(BF16)


def _neg_abs(x):
    bits = lax.bitcast_convert_type(x, jnp.uint32) | jnp.uint32(0x80000000)
    return lax.bitcast_convert_type(bits, F32)


def _sb_scores(q, kblk, *, diagonal):
    z2 = lax.dot_general(q, kblk, (((2,), (2,)), ((0,), (0,))), preferred_element_type=F32)
    heads, tq, tk = z2.shape
    sp2 = jnp.maximum(z2, 0.0) + jnp.log2(1.0 + jnp.exp2(_neg_abs(z2)))
    if diagonal:
        row = lax.broadcasted_iota(jnp.int32, (1, tq, tk), 1)
        col = lax.broadcasted_iota(jnp.int32, (1, tq, tk), 2)
        mask = col < row
        sp2 = jnp.where(mask, sp2, 0.0)
        z2 = jnp.where(mask, z2, -jnp.inf)
    hi = sp2.astype(BF16)
    lo = (sp2 - hi.astype(F32)).astype(BF16)
    parts = jnp.concatenate([hi, lo], axis=-1).reshape(heads * tq, 2 * tk)
    return z2, parts


def _sb_accumulate(z2, parts, vblk, neg_incl, run, acc):
    heads, tq, tk = z2.shape
    incl = jnp.dot(parts, neg_incl, preferred_element_type=F32).reshape(heads, tq, tk)
    a = jnp.exp2((z2 + run) + incl)
    acc = acc + lax.dot_general(a.astype(BF16), vblk, (((2,), (1,)), ((0,), (0,))), preferred_element_type=F32)
    run = run + incl[:, :, 0:1]
    return run, acc


def _head_rmsnorm(x, g):
    return x * lax.rsqrt(jnp.mean(x * x, axis=-1, keepdims=True) + EPS) * g


def _sb_prompt_kernel(q_ref, k_ref, v_ref, g_ref, mixed_ref, o_ref, *, tile, heads, tiles_per_trip):
    del mixed_ref
    qi = pl.program_id(2)
    d = SB_HEAD_DIM
    neg_incl = _neg_inclusive_matrix(tile)
    q = jnp.stack([q_ref[:, h * d:(h + 1) * d] for h in range(heads)])

    def key_rows(kb):
        return pl.ds(pl.multiple_of(kb * tile, tile), tile)

    def visit(kb, carry, diagonal):
        kblk = jnp.stack([k_ref[key_rows(kb), h * d:(h + 1) * d] for h in range(heads)])
        vblk = jnp.stack([v_ref[key_rows(kb), h * d:(h + 1) * d] for h in range(heads)])
        z2, parts = _sb_scores(q, kblk, diagonal=diagonal)
        return _sb_accumulate(z2, parts, vblk, neg_incl, *carry)

    def visit_group(kb, carry):
        for g in range(tiles_per_trip):
            carry = visit(kb - g, carry, False)
        return carry

    carry = (jnp.zeros((heads, tile, 1), F32), jnp.zeros((heads, tile, d), F32))
    carry = visit(qi, carry, True)
    singles = qi % tiles_per_trip
    carry = lax.fori_loop(0, singles, lambda t, c: visit(qi - 1 - t, c, False), carry)
    first = qi - 1 - singles
    carry = lax.fori_loop(0, qi // tiles_per_trip, lambda t, c: visit_group(first - tiles_per_trip * t, c), carry)
    for h in range(heads):
        cols = slice(h * d, (h + 1) * d)
        o_ref[:, cols] = _head_rmsnorm(carry[1][h], g_ref[:, cols]).astype(o_ref.dtype)


def _sb_prompt(q, k, v, g_heads, mixed, *, batch, seq, tile, heads_per_step):
    d = SB_HEAD_DIM
    w = heads_per_step * d
    groups = k.shape[1] // w
    nq = seq // tile
    blocks = (2 * _nbytes((tile, w), BF16) + 2 * _nbytes((seq, w), BF16) + _nbytes((8, w), F32))
    return pl.pallas_call(
        functools.partial(_sb_prompt_kernel, tile=tile, heads=heads_per_step, tiles_per_trip=SB_TILES_PER_TRIP),
        out_shape=jax.ShapeDtypeStruct(mixed.shape, mixed.dtype),
        grid=(batch, groups, nq),
        in_specs=[pl.BlockSpec((tile, w), lambda b, h, i: (b * nq + i, h)),
                  pl.BlockSpec((seq, w), lambda b, h, i: (b, h)),
                  pl.BlockSpec((seq, w), lambda b, h, i: (b, h)),
                  pl.BlockSpec((None, 1, w), lambda b, h, i: (h, 0, 0)),
                  pl.BlockSpec(memory_space=pl.ANY)],
        out_specs=pl.BlockSpec((tile, w), lambda b, h, i: (b * nq + i, h)),
        input_output_aliases={4: 0},
        compiler_params=pltpu.CompilerParams(
            dimension_semantics=("parallel", "parallel", "arbitrary"),
            vmem_limit_bytes=_vmem_limit(blocks, temp_bytes=16 * heads_per_step * _nbytes((tile, tile), F32))),
        name="sb_prompt",
    )(q, k, v, g_heads.reshape(groups, 1, w), mixed)


def _sb_sample_kernel(q_ref, kn_ref, vn_ref, kc_ref, vc_ref, g_ref, mixed_ref, o_ref, run_ref, acc_ref,
                      *, heads, tile):
    del mixed_ref
    kc = pl.program_id(1)
    nkc = pl.num_programs(1)
    d = SB_HEAD_DIM
    n_new = kn_ref.shape[0]
    q = jnp.stack([q_ref[:, h * d:(h + 1) * d] for h in range(heads)])

    @pl.when(kc == 0)
    def _():
        pad = jnp.zeros((heads, V7X_LANES - n_new, d), BF16)
        kblk = jnp.concatenate([jnp.stack([kn_ref[:, h * d:(h + 1) * d] for h in range(heads)]).astype(BF16), pad], axis=1)
        vblk = jnp.concatenate([jnp.stack([vn_ref[:, h * d:(h + 1) * d] for h in range(heads)]).astype(BF16), pad], axis=1)
        z2, parts = _sb_scores(q, kblk, diagonal=True)
        run, acc = _sb_accumulate(z2, parts, vblk, _neg_inclusive_matrix(V7X_LANES),
                                  jnp.zeros((heads, n_new, 1), F32), jnp.zeros((heads, n_new, d), F32))
        run_ref[...] = run
        acc_ref[...] = acc

    kblk = pltpu.einshape("phd->hpd", kc_ref[...].astype(BF16).reshape(tile, heads, d))
    vblk = pltpu.einshape("phd->hpd", vc_ref[...].astype(BF16).reshape(tile, heads, d))
    z2, parts = _sb_scores(q, kblk, diagonal=False)
    run, acc = _sb_accumulate(z2, parts, vblk, _neg_inclusive_matrix(tile), run_ref[...], acc_ref[...])
    run_ref[...] = run
    acc_ref[...] = acc

    @pl.when(kc == nkc - 1)
    def _():
        for h in range(heads):
            cols = slice(h * d, (h + 1) * d)
            o_ref[:, cols] = _head_rmsnorm(acc_ref[h], g_ref[h:h + 1, :]).astype(o_ref.dtype)


def _sb_sample(q, k_new, v_new, k_cache, v_cache, g_heads, mixed, *, layer, row_offset, tile):
    _, batch, past, heads, d = k_cache.shape
    width = heads * d
    n_new = k_new.shape[0] // batch
    roff = row_offset // n_new
    nkc = past // tile
    blocks = (2 * _nbytes((n_new, width), BF16) + 2 * _nbytes((n_new, width), F32)
              + 2 * _nbytes((tile, heads, d), F32) + _nbytes((heads, d), F32))
    state_bytes = _nbytes((heads, n_new, V7X_LANES), F32) + _nbytes((heads, n_new, d), F32)
    k_cache = k_cache.reshape(-1, batch, past * heads, d)
    v_cache = v_cache.reshape(-1, batch, past * heads, d)
    cache_spec = pl.BlockSpec((None, None, tile * heads, d), lambda b, c: (layer, b, nkc - 1 - c, 0))
    return pl.pallas_call(
        functools.partial(_sb_sample_kernel, heads=heads, tile=tile),
        out_shape=jax.ShapeDtypeStruct(mixed.shape, mixed.dtype),
        grid=(batch, nkc),
        in_specs=[pl.BlockSpec((n_new, width), lambda b, c: (b + roff, 0)),
                  pl.BlockSpec((n_new, width), lambda b, c: (b, 0)),
                  pl.BlockSpec((n_new, width), lambda b, c: (b, 0)),
                  cache_spec,
                  cache_spec,
                  pl.BlockSpec((heads, d), lambda b, c: (0, 0)),
                  pl.BlockSpec(memory_space=pl.ANY)],
        out_specs=pl.BlockSpec((n_new, width), lambda b, c: (b + roff, 0)),
        scratch_shapes=[pltpu.VMEM((heads, n_new, 1), F32), pltpu.VMEM((heads, n_new, d), F32)],
        input_output_aliases={6: 0},
        compiler_params=pltpu.CompilerParams(
            dimension_semantics=("parallel", "arbitrary"),
            vmem_limit_bytes=_vmem_limit(blocks, scratch_bytes=state_bytes, temp_bytes=12 << 20)),
        name="sb_sample",
    )(q, k_new, v_new, k_cache, v_cache, g_heads, mixed)


def _log_sigmoid(x):
    return jnp.minimum(x, 0.0) - jnp.log1p(jnp.exp(-jnp.abs(x)))


def _gate_rows(x, chunk):
    pad = (-chunk) % V7X_LANES
    if pad:
        x = jnp.concatenate([x, jnp.zeros((pad, V7X_LANES), x.dtype)], axis=0)
    return x.T[:N_GATES, :chunk]


def _mlstm_kernel(q_ref, k_ref, v_ref, o_ref, g_ref, gain_ref, c0_ref, n0_ref, m0_ref, mixed_ref,
                  out_ref, c_out_ref, n_out_ref, m_out_ref, c_sc, n_sc, m_sc, *, chunk):
    del mixed_ref
    ci = pl.program_id(1)
    nci = pl.num_programs(1)

    @pl.when(ci == 0)
    def _():
        c_sc[...] = c0_ref[...]
        n_sc[...] = n0_ref[...]
        m_sc[...] = m0_ref[...]

    row = lax.broadcasted_iota(jnp.int32, (chunk, chunk), 0)
    col = lax.broadcasted_iota(jnp.int32, (chunk, chunk), 1)
    causal = col <= row
    tri = causal.astype(BF16)

    g = g_ref[...]
    b_cols = sum(jnp.dot(tri, part, preferred_element_type=F32) for part in _split3(_log_sigmoid(g)))
    g_rows = _gate_rows(g, chunk)
    b_rows = _gate_rows(b_cols, chunk)

    for h in range(ML_HEADS):
        qk = slice(h * ML_QK_DIM, (h + 1) * ML_QK_DIM)
        vv = slice(h * ML_V_DIM, (h + 1) * ML_V_DIM)
        q = q_ref[:, qk]
        k = k_ref[:, qk]
        v = v_ref[:, vv]
        c_prev = c_sc[h]
        n_prev = n_sc[h:h + 1, :]
        m_prev = m_sc[h:h + 1, 0:1]

        b_col = b_cols[:, ML_HEADS + h:ML_HEADS + h + 1]
        i_col = g[:, h:h + 1]
        b_row = b_rows[ML_HEADS + h:ML_HEADS + h + 1, :]
        i_row = g_rows[h:h + 1, :]

        d_log = jnp.where(causal, b_col - b_row + i_row, -jnp.inf)
        inter_log = b_col + m_prev
        m_row = jnp.maximum(inter_log, jnp.max(d_log, axis=-1, keepdims=True))
        w_intra = jnp.exp(d_log - m_row)
        w_inter = jnp.exp(inter_log - m_row)

        s = lax.dot_general(q, k, (((1,), (1,)), ((), ())), preferred_element_type=F32) * w_intra
        inter = lax.dot_general(q, c_prev.astype(BF16), (((1,), (1,)), ((), ())), preferred_element_type=F32)
        num = w_inter * inter + jnp.dot(s.astype(BF16), v, preferred_element_type=F32)
        den = w_inter * jnp.sum(q.astype(F32) * n_prev, axis=-1, keepdims=True) + jnp.sum(s, axis=-1, keepdims=True)
        hid = num / jnp.maximum(jnp.abs(den), jnp.exp(-m_row))

        m_new = m_row[chunk - 1:chunk, :]
        b_last = b_col[chunk - 1:chunk, :]
        w_prev = jnp.exp(b_last + m_prev - m_new)
        w_tok = jnp.exp(b_last - b_col + i_col - m_new)
        vw = (v.astype(F32) * w_tok).astype(BF16)
        c_sc[h] = w_prev * c_prev + lax.dot_general(vw, k, (((0,), (0,)), ((), ())), preferred_element_type=F32)
        n_sc[h:h + 1, :] = w_prev * n_prev + jnp.sum(k.astype(F32) * w_tok, axis=0, keepdims=True)
        m_sc[h:h + 1, :] = jnp.broadcast_to(m_new, (1, m_sc.shape[1]))

        normed = _head_rmsnorm(hid, gain_ref[h:h + 1, :])
        out_ref[:, vv] = (normed * _sigmoid(o_ref[:, vv])).astype(out_ref.dtype)

    @pl.when(ci == nci - 1)
    def _():
        c_out_ref[...] = c_sc[...]
        n_out_ref[...] = n_sc[...]
        m_out_ref[...] = m_sc[...]


def _mlstm(q, k, v, o, g, gain, c0, n0, m0, mixed, *, batch, seq, chunk, row_offset, col_offset):
    nci = seq // chunk
    roff = row_offset // chunk
    vw = ML_HEADS * ML_V_DIM
    qw = ML_HEADS * ML_QK_DIM
    coff = col_offset // vw
    state_bytes = (_nbytes((ML_HEADS, ML_V_DIM, ML_QK_DIM), F32) + _nbytes((ML_HEADS, ML_QK_DIM), F32)
                   + _nbytes((ML_HEADS, V7X_LANES), F32))
    blocks = (2 * _nbytes((chunk, qw), BF16) + 2 * _nbytes((chunk, vw), BF16) + _nbytes((chunk, vw), F32)
              + _nbytes((chunk, V7X_LANES), F32) + 2 * state_bytes)
    row_map = lambda b, c: (b * nci + c + roff, 0)
    state_specs = [pl.BlockSpec((None, ML_HEADS, ML_V_DIM, ML_QK_DIM), lambda b, c: (b, 0, 0, 0)),
                   pl.BlockSpec((None, ML_HEADS, ML_QK_DIM), lambda b, c: (b, 0, 0)),
                   pl.BlockSpec((None, ML_HEADS, V7X_LANES), lambda b, c: (b, 0, 0))]
    return pl.pallas_call(
        functools.partial(_mlstm_kernel, chunk=chunk),
        out_shape=[jax.ShapeDtypeStruct(mixed.shape, mixed.dtype),
                   jax.ShapeDtypeStruct(c0.shape, F32),
                   jax.ShapeDtypeStruct(n0.shape, F32),
                   jax.ShapeDtypeStruct(m0.shape, F32)],
        grid=(batch, nci),
        in_specs=[pl.BlockSpec((chunk, qw), row_map),
                  pl.BlockSpec((chunk, qw), row_map),
                  pl.BlockSpec((chunk, vw), row_map),
                  pl.BlockSpec((chunk, vw), row_map),
                  pl.BlockSpec((chunk, V7X_LANES), row_map),
                  pl.BlockSpec((ML_HEADS, ML_V_DIM), lambda b, c: (0, 0)),
                  *state_specs,
                  pl.BlockSpec(memory_space=pl.ANY)],
        out_specs=[pl.BlockSpec((chunk, vw), lambda b, c: (b * nci + c + roff, coff)), *state_specs],
        scratch_shapes=[pltpu.VMEM((ML_HEADS, ML_V_DIM, ML_QK_DIM), F32),
                        pltpu.VMEM((ML_HEADS, ML_QK_DIM), F32),
                        pltpu.VMEM((ML_HEADS, V7X_LANES), F32)],
        input_output_aliases={9: 0},
        compiler_params=pltpu.CompilerParams(
            dimension_semantics=("parallel", "arbitrary"),
            vmem_limit_bytes=_vmem_limit(blocks, scratch_bytes=state_bytes, temp_bytes=12 << 20)),
        name="mlstm",
    )(q, k, v, o, g, gain, c0, n0, m0, mixed)


def _pick_row_tile(t, target):
    best = 16
    for cand in range(16, target + 1, 16):
        if t % cand == 0:
            best = cand
    return best


def kernel(x_prompt, x_sample, cache_sb_k, cache_sb_v, state_ml_c, state_ml_n, state_ml_m, p_prompt, p_sample, ln_ffn1, w_ffn1_gate, w_ffn1_up, w_ffn1_down, ln_mix, w_in, b_if, g_sb_head, g_ml_head, w_out, ln_ffn2, w_ffn2_gate, w_ffn2_up, w_ffn2_down, ln_ple, w_ple_gate, w_ple_proj, ln_final):
    depth = w_in.shape[0]
    batch, seq, d_model = x_prompt.shape
    dec_batch, dec_seq, _ = x_sample.shape
    tp = batch * seq
    ts = dec_batch * dec_seq
    t_all = tp + ts
    sb_heads = g_sb_head.shape[1]
    sb_width = sb_heads * SB_HEAD_DIM
    ml_qk_width = ML_HEADS * ML_QK_DIM
    ml_v_width = ML_HEADS * ML_V_DIM
    mix_width = sb_width + ml_v_width
    d_ff = w_ffn1_gate.shape[2]
    off_q, off_k, off_v = 0, sb_width, 2 * sb_width
    off_mq = 3 * sb_width
    off_mk = off_mq + ml_qk_width
    off_mv = off_mk + ml_qk_width
    off_mo = off_mv + ml_v_width
    off_g = off_mo + ml_v_width

    tm_tall = _pick_row_tile(t_all, MATMUL_ROW_TILE_TARGET)
    tm_prompt = _pick_row_tile(tp, MATMUL_ROW_TILE_TARGET)
    norm_tile = _pick_row_tile(t_all, NORM_ROW_TILE_TARGET)
    mm_tn = V7X_MXU_DIM
    sb_tile = V7X_MXU_DIM
    ml_chunk = min(seq, V7X_MXU_DIM)
    tall = dict(tm=tm_tall, single_buffer_lhs=True)

    x, h = _stack_tokens(x_prompt.reshape(tp, d_model), x_sample.reshape(ts, d_model), ln_ffn1[0],
                         tile=math.gcd(tp, ts))

    def half_ffn(x, h, w_gate, w_up, w_down):
        a = _matmul_fullk(h, [w_gate, w_up], [0, 0], d_ff, [BF16], _ep_half_swiglu, tn=mm_tn,
                          name="ffn_gate_up", **tall)[0]
        return _ffn_down(a, w_down, x, tm=tm_tall, tn=DOWN_COL_TILE, tk=DOWN_K_TILE)

    outs = {name: [] for name in ("pk", "pv", "pc", "pn", "pm", "sk", "sv", "sc", "sn", "sm")}
    for l in range(depth):
        if l > 0:
            h = _rmsnorm(x, ln_ffn1[l], BF16, rows=t_all, row_tile=norm_tile)
        x = half_ffn(x, h, w_ffn1_gate[l], w_ffn1_up[l], w_ffn1_down[l])

        h = _rmsnorm(x, ln_mix[l], BF16, rows=t_all, row_tile=norm_tile)
        w = jnp.swapaxes(w_in[l], 0, 1)
        proj = functools.partial(_matmul_fullk, h, w_transposed=True)
        (sb_q,) = proj([w], [off_q], sb_width, [BF16], _ep_sb_q, tm=tm_tall, tn=mm_tn, name="proj_sb_q")
        k_p, v_p, k_pb, v_pb = proj([w, w], [off_k, off_v], sb_width, [F32, F32, BF16, BF16], _ep_kv,
                                    tm=tm_prompt, single_buffer_lhs=True, tn=mm_tn, rows=tp, name="proj_sb_kv_prompt")
        k_s, v_s = proj([w, w], [off_k, off_v], sb_width, [F32, F32], _ep_identity,
                        tm=ts, tn=mm_tn, rows=ts, row_offset=tp, name="proj_sb_kv_sample")
        ml_q, ml_k = proj([w, w], [off_mq, off_mk], ml_qk_width, [BF16, BF16], _ep_ml_qk,
                          tn=mm_tn, name="proj_ml_qk", **tall)
        ml_v, ml_o = proj([w, w], [off_mv, off_mo], ml_v_width, [BF16, F32], _ep_identity,
                          tn=mm_tn, name="proj_ml_vo", **tall)
        b_row = jnp.pad(b_if[l], (0, V7X_LANES - N_GATES)).reshape(1, V7X_LANES)
        (gates,) = proj([w], [off_g], V7X_LANES, [F32], _ep_gates, extras=(b_row,),
                        tm=tm_tall, tn=V7X_LANES, name="proj_ml_gates")

        assert h.shape == (t_all, mix_width) and h.dtype == BF16
        mixed = _sb_prompt(sb_q, k_pb, v_pb, g_sb_head[l], h, batch=batch, seq=seq,
                           tile=sb_tile, heads_per_step=SB_HEADS_PER_STEP)
        mixed = _sb_sample(sb_q, k_s, v_s, cache_sb_k, cache_sb_v, g_sb_head[l], mixed,
                           layer=l, row_offset=tp, tile=sb_tile)
        c0 = jnp.zeros((batch, ML_HEADS, ML_V_DIM, ML_QK_DIM), F32)
        n0 = jnp.zeros((batch, ML_HEADS, ML_QK_DIM), F32)
        m0 = jnp.zeros((batch, ML_HEADS, V7X_LANES), F32)
        mixed, c_p, n_p, m_p = _mlstm(ml_q, ml_k, ml_v, ml_o, gates, g_ml_head[l], c0, n0, m0, mixed,
                                      batch=batch, seq=seq, chunk=ml_chunk, row_offset=0, col_offset=sb_width)
        m0_s = jnp.broadcast_to(state_ml_m[l][:, :, None], (dec_batch, ML_HEADS, V7X_LANES))
        mixed, c_s, n_s, m_s = _mlstm(ml_q, ml_k, ml_v, ml_o, gates, g_ml_head[l], state_ml_c[l], state_ml_n[l],
                                      m0_s, mixed, batch=dec_batch, seq=dec_seq, chunk=dec_seq,
                                      row_offset=tp, col_offset=sb_width)

        (x,) = _matmul_fullk(mixed, [w_out[l]], [0], d_model, [F32], _ep_residual, extras=(x,),
                             tn=mm_tn, name="out_proj", **tall)
        h = _rmsnorm(x, ln_ffn2[l], BF16, rows=t_all, row_tile=norm_tile)
        x = half_ffn(x, h, w_ffn2_gate[l], w_ffn2_up[l], w_ffn2_down[l])

        h = _rmsnorm(x, ln_ple[l], BF16, rows=t_all, row_tile=norm_tile)
        p = jnp.concatenate([p_prompt[l].reshape(tp, -1), p_sample[l].reshape(ts, -1)], axis=0)
        x = _ple(h, w_ple_gate[l], p, w_ple_proj[l], x, tm=tm_tall, tn=mm_tn)

        outs["pk"].append(k_p.reshape(batch, seq, sb_heads, SB_HEAD_DIM))
        outs["pv"].append(v_p.reshape(batch, seq, sb_heads, SB_HEAD_DIM))
        outs["pc"].append(c_p)
        outs["pn"].append(n_p)
        outs["pm"].append(m_p[:, :, 0])
        outs["sk"].append(k_s.reshape(dec_batch, dec_seq, sb_heads, SB_HEAD_DIM))
        outs["sv"].append(v_s.reshape(dec_batch, dec_seq, sb_heads, SB_HEAD_DIM))
        outs["sc"].append(c_s)
        outs["sn"].append(n_s)
        outs["sm"].append(m_s[:, :, 0])

    y_prompt = _rmsnorm(x, ln_final, F32, rows=tp, row_tile=FINAL_NORM_ROW_TILE).reshape(batch, seq, d_model)
    y_sample = _rmsnorm(x, ln_final, F32, rows=ts, row_tile=ts, row_offset=tp).reshape(dec_batch, dec_seq, d_model)

    def stack(name):
        vals = outs[name]
        return vals[0][None] if len(vals) == 1 else jnp.stack(vals)

    return (y_prompt, y_sample, stack("pk"), stack("pv"), stack("pc"), stack("pn"), stack("pm"),
            stack("sk"), stack("sv"), stack("sc"), stack("sn"), stack("sm"))
```

```python
import functools
import math

import jax
import jax.numpy as jnp
from jax import lax
from jax.experimental import pallas as pl
from jax.experimental.pallas import tpu as pltpu

F32 = jnp.float32
BF16 = jnp.bfloat16
EPS = 1e-6
LOG2_E = math.log2(math.e)

V7X_VMEM_BYTES = 64 * 1024 * 1024
V7X_LANES = 128
V7X_MXU_DIM = 256
VMEM_REQUEST_CAP = V7X_VMEM_BYTES - 6 * 1024 * 1024

SB_HEAD_DIM = 128
ML_QK_DIM = 256
ML_V_DIM = 512
ML_HEADS = 4
N_GATES = 2 * ML_HEADS
SB_Q_SCALE = LOG2_E * SB_HEAD_DIM ** -0.5
SB_TILES_PER_TRIP = 4
SB_HEADS_PER_STEP = 8

MATMUL_ROW_TILE_TARGET = 2176
NORM_ROW_TILE_TARGET = 384
FINAL_NORM_ROW_TILE = 256
DOWN_COL_TILE = 1024
DOWN_K_TILE = 1024


def _vmem_limit(block_bytes, scratch_bytes=0, temp_bytes=0):
    est = 2 * block_bytes + scratch_bytes + temp_bytes + (2 << 20)
    return int(min(max(est, 16 << 20), VMEM_REQUEST_CAP))


def _nbytes(shape, dtype):
    n = 1
    for s in shape:
        n *= s
    return n * jnp.dtype(dtype).itemsize


def _sigmoid(x):
    return 1.0 / (1.0 + jnp.exp(-x))


def _rmsnorm_kernel(x_ref, g_ref, o_ref):
    x = x_ref[...]
    y = x * lax.rsqrt(jnp.mean(x * x, axis=-1, keepdims=True) + EPS)
    o_ref[...] = (y * g_ref[...]).astype(o_ref.dtype)


def _rmsnorm(x, g, out_dtype, *, rows, row_tile, row_offset=0):
    d = x.shape[1]
    off = row_offset // row_tile
    blocks = _nbytes((row_tile, d), F32) + _nbytes((row_tile, d), out_dtype)
    return pl.pallas_call(
        _rmsnorm_kernel,
        out_shape=jax.ShapeDtypeStruct((rows, d), out_dtype),
        grid=(rows // row_tile,),
        in_specs=[pl.BlockSpec((row_tile, d), lambda i: (i + off, 0)),
                  pl.BlockSpec((1, d), lambda i: (0, 0))],
        out_specs=pl.BlockSpec((row_tile, d), lambda i: (i, 0)),
        compiler_params=pltpu.CompilerParams(
            dimension_semantics=("parallel",),
            vmem_limit_bytes=_vmem_limit(blocks, temp_bytes=2 * _nbytes((row_tile, d), F32))),
        name="rmsnorm",
    )(x, g.reshape(1, d))


def _stack_kernel(xp_ref, xs_ref, g_ref, x_ref, h_ref, *, n_prompt_tiles):
    i = pl.program_id(0)

    def emit(src_ref):
        x_ref[...] = src_ref[...]
        _rmsnorm_kernel(src_ref, g_ref, h_ref)

    pl.when(i < n_prompt_tiles)(lambda: emit(xp_ref))
    pl.when(i >= n_prompt_tiles)(lambda: emit(xs_ref))


def _stack_tokens(x_prompt, x_sample, gain, *, tile):
    tp, d = x_prompt.shape
    ts = x_sample.shape[0]
    n_p, n_s = tp // tile, ts // tile
    t_all = tp + ts
    blocks = 3 * _nbytes((tile, d), F32) + _nbytes((tile, d), BF16)
    row = lambda i: (i, 0)
    return pl.pallas_call(
        functools.partial(_stack_kernel, n_prompt_tiles=n_p),
        out_shape=[jax.ShapeDtypeStruct((t_all, d), F32), jax.ShapeDtypeStruct((t_all, d), BF16)],
        grid=(n_p + n_s,),
        in_specs=[pl.BlockSpec((tile, d), lambda i: (jnp.minimum(i, n_p - 1), 0)),
                  pl.BlockSpec((tile, d), lambda i: (jnp.maximum(i - n_p, 0), 0)),
                  pl.BlockSpec((1, d), lambda i: (0, 0))],
        out_specs=[pl.BlockSpec((tile, d), row), pl.BlockSpec((tile, d), row)],
        compiler_params=pltpu.CompilerParams(
            dimension_semantics=("parallel",),
            vmem_limit_bytes=_vmem_limit(blocks, temp_bytes=2 * _nbytes((tile, d), F32))),
        name="stack_tokens",
    )(x_prompt, x_sample, gain.reshape(1, d))


def _mm_kernel(*refs, n_w, n_extra, epilogue, w_transposed):
    lhs_ref = refs[0]
    w_refs = refs[1:1 + n_w]
    extra_refs = refs[1 + n_w:1 + n_w + n_extra]
    out_refs = refs[1 + n_w + n_extra:]
    lhs = lhs_ref[...]
    w_contract = 1 if w_transposed else 0
    accs = [lax.dot_general(lhs, w[...].astype(BF16), (((1,), (w_contract,)), ((), ())),
                            preferred_element_type=F32) for w in w_refs]
    outs = epilogue(accs, [e[...] for e in extra_refs])
    for o_ref, o in zip(out_refs, outs):
        o_ref[...] = o.astype(o_ref.dtype)


def _matmul_fullk(lhs, ws, col_offsets, n_cols, out_dtypes, epilogue, extras=(), *,
                  tm, tn, rows=None, row_offset=0, w_transposed=False, single_buffer_lhs=False, name):
    k_dim = lhs.shape[1]
    rows = lhs.shape[0] if rows is None else rows
    roff = row_offset // tm
    lhs_mode = dict(pipeline_mode=pl.Buffered(1)) if single_buffer_lhs else {}
    in_specs = [pl.BlockSpec((tm, k_dim), lambda i, j: (i + roff, 0), **lhs_mode)]
    for off in col_offsets:
        if w_transposed:
            in_specs.append(pl.BlockSpec((tn, k_dim), lambda i, j, o=off // tn: (j + o, 0)))
        else:
            in_specs.append(pl.BlockSpec((k_dim, tn), lambda i, j, o=off // tn: (0, j + o)))
    extra_bytes = 0
    for e in extras:
        if e.shape[0] == 1:
            in_specs.append(pl.BlockSpec((1, tn), lambda i, j: (0, j)))
            extra_bytes += _nbytes((8, tn), F32)
        else:
            in_specs.append(pl.BlockSpec((tm, tn), lambda i, j: (i + roff, j)))
            extra_bytes += _nbytes((tm, tn), F32)
    out_specs = [pl.BlockSpec((tm, tn), lambda i, j: (i, j)) for _ in out_dtypes]
    out_shape = [jax.ShapeDtypeStruct((rows, n_cols), dt) for dt in out_dtypes]
    out_bytes = sum(_nbytes((tm, tn), dt) for dt in out_dtypes)
    blocks = _nbytes((tm, k_dim), BF16) + len(ws) * _nbytes((k_dim, tn), F32) + extra_bytes + out_bytes
    temps = (len(ws) * ((3 if w_transposed else 1) * _nbytes((k_dim, tn), BF16) + 4 * _nbytes((tm, tn), F32))
             + len(extras) * _nbytes((tm, tn), F32))
    if single_buffer_lhs:
        temps -= _nbytes((tm, k_dim), BF16)
    return pl.pallas_call(
        functools.partial(_mm_kernel, n_w=len(ws), n_extra=len(extras), epilogue=epilogue,
                          w_transposed=w_transposed),
        out_shape=out_shape,
        grid=(rows // tm, n_cols // tn),
        in_specs=in_specs,
        out_specs=out_specs,
        compiler_params=pltpu.CompilerParams(
            dimension_semantics=("parallel", "arbitrary"),
            vmem_limit_bytes=_vmem_limit(blocks, temp_bytes=temps)),
        name=name,
    )(lhs, *ws, *extras)


def _ep_half_swiglu(accs, extras):
    g, u = accs
    return [0.5 * (g * _sigmoid(g) * u)]


def _ep_identity(accs, extras):
    return list(accs)


def _ep_kv(accs, extras):
    k, v = accs
    return [k, v, k, v]


def _ep_residual(accs, extras):
    return [extras[0] + accs[0]]


def _ep_gates(accs, extras):
    lane = lax.broadcasted_iota(jnp.int32, accs[0].shape, 1)
    return [jnp.where(lane < N_GATES, accs[0] + extras[0], 0.0)]


def _ep_sb_q(accs, extras):
    return [accs[0] * SB_Q_SCALE]


def _ep_ml_qk(accs, extras):
    q, k = accs
    return [q * (ML_QK_DIM ** -0.5), k]


def _gate_up_kernel(h_ref, wg_ref, wu_ref, a_ref, *, n_real):
    j = pl.program_id(1)

    @pl.when(j < n_real)
    def _():
        h = h_ref[...]
        g = jnp.dot(h, wg_ref[...].astype(BF16), preferred_element_type=F32)
        u = jnp.dot(h, wu_ref[...].astype(BF16), preferred_element_type=F32)
        a_ref[...] = _ep_half_swiglu([g, u], [])[0].astype(a_ref.dtype)

    @pl.when(j >= n_real)
    def _():
        a_ref[...] = jnp.zeros_like(a_ref)


def _ffn_gate_up(h, w_gate, w_up, *, tm, tn, tk):
    t, d = h.shape
    f = w_gate.shape[1]
    n_real = f // tn
    per_k = tk // tn
    nk = pl.cdiv(f, tk)
    w_map = lambda i, j: (0, jnp.minimum(j, n_real - 1))
    blocks = 2 * _nbytes((d, tn), F32) + _nbytes((tm, tn), BF16)
    temps = 2 * (_nbytes((d, tn), BF16) + 4 * _nbytes((tm, tn), F32))
    return pl.pallas_call(
        functools.partial(_gate_up_kernel, n_real=n_real),
        out_shape=jax.ShapeDtypeStruct((nk, t, tk), BF16),
        grid=(t // tm, nk * per_k),
        in_specs=[pl.BlockSpec((tm, d), lambda i, j: (i, 0), pipeline_mode=pl.Buffered(1)),
                  pl.BlockSpec((d, tn), w_map),
                  pl.BlockSpec((d, tn), w_map)],
        out_specs=pl.BlockSpec((None, tm, tn), lambda i, j: (j // per_k, i, j % per_k)),
        compiler_params=pltpu.CompilerParams(
            dimension_semantics=("parallel", "arbitrary"),
            vmem_limit_bytes=_vmem_limit(blocks, scratch_bytes=_nbytes((tm, d), BF16), temp_bytes=temps)),
        name="ffn_gate_up",
    )(h, w_gate, w_up)


def _down_kernel(a_ref, w_ref, x_ref, o_ref, *, nk, k_valid_last, tk):
    k = pl.program_id(2)

    def partial_product(masked):
        a = a_ref[...]
        w = w_ref[...]
        if masked:
            w = jnp.where(lax.broadcasted_iota(jnp.int32, w.shape, 0) < k_valid_last, w, jnp.zeros_like(w))
        return jnp.dot(a, w.astype(BF16), preferred_element_type=F32)

    @pl.when(k == 0)
    def _():
        o_ref[...] = x_ref[...] + partial_product(nk == 1 and k_valid_last != tk)

    if nk > 1:
        if k_valid_last == tk:
            @pl.when(k > 0)
            def _():
                o_ref[...] += partial_product(False)
        else:
            @pl.when(jnp.logical_and(k > 0, k < nk - 1))
            def _():
                o_ref[...] += partial_product(False)

            @pl.when(k == nk - 1)
            def _():
                o_ref[...] += partial_product(True)


def _ffn_down(a, w_down, x, *, tm, tn):
    nk, t, tk = a.shape
    f, d = w_down.shape
    k_valid_last = f - (nk - 1) * tk
    blocks = (_nbytes((tm, tk), BF16) + _nbytes((tk, tn), F32) + 2 * _nbytes((tm, tn), F32))
    return pl.pallas_call(
        functools.partial(_down_kernel, nk=nk, k_valid_last=k_valid_last, tk=tk),
        out_shape=jax.ShapeDtypeStruct((t, d), F32),
        grid=(t // tm, d // tn, nk),
        in_specs=[pl.BlockSpec((None, tm, tk), lambda i, j, k: (k, i, 0)),
                  pl.BlockSpec((tk, tn), lambda i, j, k: (k, j)),
                  pl.BlockSpec((tm, tn), lambda i, j, k: (i, j))],
        out_specs=pl.BlockSpec((tm, tn), lambda i, j, k: (i, j)),
        compiler_params=pltpu.CompilerParams(
            dimension_semantics=("parallel", "parallel", "arbitrary"),
            vmem_limit_bytes=_vmem_limit(blocks, temp_bytes=_nbytes((tk, tn), BF16) + (4 << 20))),
        name="ffn_down",
    )(a, w_down, x)


def _ple_kernel(h_ref, wg_ref, p_ref, wp_ref, x_ref, o_ref):
    gate = _sigmoid(jnp.dot(h_ref[...], wg_ref[...].astype(BF16), preferred_element_type=F32))
    proj = jnp.dot(p_ref[...].astype(BF16), wp_ref[...].astype(BF16), preferred_element_type=F32)
    o_ref[...] = x_ref[...] + gate * proj


def _ple(h, w_gate, p, w_proj, x, *, tm, tn):
    t, d = h.shape
    pd = p.shape[1]
    row_blocks = _nbytes((tm, d), BF16) + _nbytes((tm, pd), F32)
    blocks = _nbytes((d, tn), F32) + _nbytes((pd, tn), F32) + 2 * _nbytes((tm, tn), F32)
    once = dict(pipeline_mode=pl.Buffered(1))
    return pl.pallas_call(
        _ple_kernel,
        out_shape=jax.ShapeDtypeStruct((t, d), F32),
        grid=(t // tm, d // tn),
        in_specs=[pl.BlockSpec((tm, d), lambda i, j: (i, 0), **once),
                  pl.BlockSpec((d, tn), lambda i, j: (0, j)),
                  pl.BlockSpec((tm, pd), lambda i, j: (i, 0), **once),
                  pl.BlockSpec((pd, tn), lambda i, j: (0, j)),
                  pl.BlockSpec((tm, tn), lambda i, j: (i, j))],
        out_specs=pl.BlockSpec((tm, tn), lambda i, j: (i, j)),
        compiler_params=pltpu.CompilerParams(
            dimension_semantics=("parallel", "arbitrary"),
            vmem_limit_bytes=_vmem_limit(blocks, scratch_bytes=row_blocks,
                                         temp_bytes=_nbytes((d, tn), BF16) + 3 * _nbytes((tm, tn), F32))),
        name="ple",
    )(h, w_gate, p, w_proj, x)


def _split3(x):
    hi = x.astype(BF16)
    r = x - hi.astype(F32)
    mid = r.astype(BF16)
    lo = (r - mid.astype(F32)).astype(BF16)
    return hi, mid, lo


def _neg_inclusive_matrix(tk):
    r = lax.broadcasted_iota(jnp.int32, (2 * tk, tk), 0)
    c = lax.broadcasted_iota(jnp.int32, (2 * tk, tk), 1)
    r = jnp.where(r >= tk, r - tk, r)
    return jnp.where(r >= c, -1.0, 0.0).astype(BF16)


def _neg_abs(x):
    bits = lax.bitcast_convert_type(x, jnp.uint32) | jnp.uint32(0x80000000)
    return lax.bitcast_convert_type(bits, F32)


def _sb_scores(q, kblk, *, diagonal):
    z2 = lax.dot_general(q, kblk, (((2,), (2,)), ((0,), (0,))), preferred_element_type=F32)
    heads, tq, tk = z2.shape
    sp2 = jnp.maximum(z2, 0.0) + jnp.log2(1.0 + jnp.exp2(_neg_abs(z2)))
    if diagonal:
        row = lax.broadcasted_iota(jnp.int32, (1, tq, tk), 1)
        col = lax.broadcasted_iota(jnp.int32, (1, tq, tk), 2)
        mask = col < row
        sp2 = jnp.where(mask, sp2, 0.0)
        z2 = jnp.where(mask, z2, -jnp.inf)
    hi = sp2.astype(BF16)
    lo = (sp2 - hi.astype(F32)).astype(BF16)
    parts = jnp.concatenate([hi, lo], axis=-1).reshape(heads * tq, 2 * tk)
    return z2, parts


def _sb_accumulate(z2, parts, vblk, neg_incl, run, acc):
    heads, tq, tk = z2.shape
    incl = jnp.dot(parts, neg_incl, preferred_element_type=F32).reshape(heads, tq, tk)
    a = jnp.exp2((z2 + run) + incl)
    acc = acc + lax.dot_general(a.astype(BF16), vblk, (((2,), (1,)), ((0,), (0,))), preferred_element_type=F32)
    run = run + incl[:, :, 0:1]
    return run, acc


def _head_rmsnorm(x, g):
    return x * lax.rsqrt(jnp.mean(x * x, axis=-1, keepdims=True) + EPS) * g


def _sb_prompt_kernel(q_ref, k_ref, v_ref, g_ref, mixed_ref, o_ref, *, tile, heads, tiles_per_trip):
    del mixed_ref
    qi = pl.program_id(2)
    d = SB_HEAD_DIM
    neg_incl = _neg_inclusive_matrix(tile)
    q = jnp.stack([q_ref[:, h * d:(h + 1) * d] for h in range(heads)])

    def key_rows(kb):
        return pl.ds(pl.multiple_of(kb * tile, tile), tile)

    def visit(kb, carry, diagonal):
        kblk = jnp.stack([k_ref[key_rows(kb), h * d:(h + 1) * d] for h in range(heads)])
        vblk = jnp.stack([v_ref[key_rows(kb), h * d:(h + 1) * d] for h in range(heads)])
        z2, parts = _sb_scores(q, kblk, diagonal=diagonal)
        return _sb_accumulate(z2, parts, vblk, neg_incl, *carry)

    def visit_group(kb, carry):
        for g in range(tiles_per_trip):
            carry = visit(kb - g, carry, False)
        return carry

    carry = (jnp.zeros((heads, tile, 1), F32), jnp.zeros((heads, tile, d), F32))
    carry = visit(qi, carry, True)
    singles = qi % tiles_per_trip
    carry = lax.fori_loop(0, singles, lambda t, c: visit(qi - 1 - t, c, False), carry)
    first = qi - 1 - singles
    carry = lax.fori_loop(0, qi // tiles_per_trip, lambda t, c: visit_group(first - tiles_per_trip * t, c), carry)
    for h in range(heads):
        cols = slice(h * d, (h + 1) * d)
        o_ref[:, cols] = _head_rmsnorm(carry[1][h], g_ref[:, cols]).astype(o_ref.dtype)


def _sb_prompt(q, k, v, g_heads, mixed, *, batch, seq, tile, heads_per_step):
    d = SB_HEAD_DIM
    w = heads_per_step * d
    groups = k.shape[1] // w
    nq = seq // tile
    blocks = (2 * _nbytes((tile, w), BF16) + 2 * _nbytes((seq, w), BF16) + _nbytes((8, w), F32))
    return pl.pallas_call(
        functools.partial(_sb_prompt_kernel, tile=tile, heads=heads_per_step, tiles_per_trip=SB_TILES_PER_TRIP),
        out_shape=jax.ShapeDtypeStruct(mixed.shape, mixed.dtype),
        grid=(batch, groups, nq),
        in_specs=[pl.BlockSpec((tile, w), lambda b, h, i: (b * nq + i, h)),
                  pl.BlockSpec((seq, w), lambda b, h, i: (b, h)),
                  pl.BlockSpec((seq, w), lambda b, h, i: (b, h)),
                  pl.BlockSpec((None, 1, w), lambda b, h, i: (h, 0, 0)),
                  pl.BlockSpec(memory_space=pl.ANY)],
        out_specs=pl.BlockSpec((tile, w), lambda b, h, i: (b * nq + i, h)),
        input_output_aliases={4: 0},
        compiler_params=pltpu.CompilerParams(
            dimension_semantics=("parallel", "parallel", "arbitrary"),
            vmem_limit_bytes=_vmem_limit(blocks, temp_bytes=16 * heads_per_step * _nbytes((tile, tile), F32))),
        name="sb_prompt",
    )(q, k, v, g_heads.reshape(groups, 1, w), mixed)


def _sb_sample_kernel(q_ref, kn_ref, vn_ref, kc_ref, vc_ref, g_ref, mixed_ref, o_ref, run_ref, acc_ref,
                      *, heads, tile):
    del mixed_ref
    kc = pl.program_id(1)
    nkc = pl.num_programs(1)
    d = SB_HEAD_DIM
    n_new = kn_ref.shape[0]
    q = jnp.stack([q_ref[:, h * d:(h + 1) * d] for h in range(heads)])

    @pl.when(kc == 0)
    def _():
        pad = jnp.zeros((heads, V7X_LANES - n_new, d), BF16)
        kblk = jnp.concatenate([jnp.stack([kn_ref[:, h * d:(h + 1) * d] for h in range(heads)]).astype(BF16), pad], axis=1)
        vblk = jnp.concatenate([jnp.stack([vn_ref[:, h * d:(h + 1) * d] for h in range(heads)]).astype(BF16), pad], axis=1)
        z2, parts = _sb_scores(q, kblk, diagonal=True)
        run, acc = _sb_accumulate(z2, parts, vblk, _neg_inclusive_matrix(V7X_LANES),
                                  jnp.zeros((heads, n_new, 1), F32), jnp.zeros((heads, n_new, d), F32))
        run_ref[...] = run
        acc_ref[...] = acc

    kblk = pltpu.einshape("phd->hpd", kc_ref[...].astype(BF16).reshape(tile, heads, d))
    vblk = pltpu.einshape("phd->hpd", vc_ref[...].astype(BF16).reshape(tile, heads, d))
    z2, parts = _sb_scores(q, kblk, diagonal=False)
    run, acc = _sb_accumulate(z2, parts, vblk, _neg_inclusive_matrix(tile), run_ref[...], acc_ref[...])
    run_ref[...] = run
    acc_ref[...] = acc

    @pl.when(kc == nkc - 1)
    def _():
        for h in range(heads):
            cols = slice(h * d, (h + 1) * d)
            o_ref[:, cols] = _head_rmsnorm(acc_ref[h], g_ref[h:h + 1, :]).astype(o_ref.dtype)


def _sb_sample(q, k_new, v_new, k_cache, v_cache, g_heads, mixed, *, layer, row_offset, tile):
    _, batch, past, heads, d = k_cache.shape
    width = heads * d
    n_new = k_new.shape[0] // batch
    roff = row_offset // n_new
    nkc = past // tile
    blocks = (2 * _nbytes((n_new, width), BF16) + 2 * _nbytes((n_new, width), F32)
              + 2 * _nbytes((tile, heads, d), F32) + _nbytes((heads, d), F32))
    state_bytes = _nbytes((heads, n_new, V7X_LANES), F32) + _nbytes((heads, n_new, d), F32)
    k_cache = k_cache.reshape(-1, batch, past * heads, d)
    v_cache = v_cache.reshape(-1, batch, past * heads, d)
    cache_spec = pl.BlockSpec((None, None, tile * heads, d), lambda b, c: (layer, b, nkc - 1 - c, 0))
    return pl.pallas_call(
        functools.partial(_sb_sample_kernel, heads=heads, tile=tile),
        out_shape=jax.ShapeDtypeStruct(mixed.shape, mixed.dtype),
        grid=(batch, nkc),
        in_specs=[pl.BlockSpec((n_new, width), lambda b, c: (b + roff, 0)),
                  pl.BlockSpec((n_new, width), lambda b, c: (b, 0)),
                  pl.BlockSpec((n_new, width), lambda b, c: (b, 0)),
                  cache_spec,
                  cache_spec,
                  pl.BlockSpec((heads, d), lambda b, c: (0, 0)),
                  pl.BlockSpec(memory_space=pl.ANY)],
        out_specs=pl.BlockSpec((n_new, width), lambda b, c: (b + roff, 0)),
        scratch_shapes=[pltpu.VMEM((heads, n_new, 1), F32), pltpu.VMEM((heads, n_new, d), F32)],
        input_output_aliases={6: 0},
        compiler_params=pltpu.CompilerParams(
            dimension_semantics=("parallel", "arbitrary"),
            vmem_limit_bytes=_vmem_limit(blocks, scratch_bytes=state_bytes, temp_bytes=12 << 20)),
        name="sb_sample",
    )(q, k_new, v_new, k_cache, v_cache, g_heads, mixed)


def _log_sigmoid(x):
    return jnp.minimum(x, 0.0) - jnp.log1p(jnp.exp(-jnp.abs(x)))


def _gate_rows(x, chunk):
    pad = (-chunk) % V7X_LANES
    if pad:
        x = jnp.concatenate([x, jnp.zeros((pad, V7X_LANES), x.dtype)], axis=0)
    return x.T[:N_GATES, :chunk]


def _mlstm_kernel(q_ref, k_ref, v_ref, o_ref, g_ref, gain_ref, c0_ref, n0_ref, m0_ref, mixed_ref,
                  out_ref, c_out_ref, n_out_ref, m_out_ref, c_sc, n_sc, m_sc, *, chunk):
    del mixed_ref
    ci = pl.program_id(1)
    nci = pl.num_programs(1)

    @pl.when(ci == 0)
    def _():
        c_sc[...] = c0_ref[...]
        n_sc[...] = n0_ref[...]
        m_sc[...] = m0_ref[...]

    row = lax.broadcasted_iota(jnp.int32, (chunk, chunk), 0)
    col = lax.broadcasted_iota(jnp.int32, (chunk, chunk), 1)
    causal = col <= row
    tri = causal.astype(BF16)

    g = g_ref[...]
    b_cols = sum(jnp.dot(tri, part, preferred_element_type=F32) for part in _split3(_log_sigmoid(g)))
    g_rows = _gate_rows(g, chunk)
    b_rows = _gate_rows(b_cols, chunk)

    for h in range(ML_HEADS):
        qk = slice(h * ML_QK_DIM, (h + 1) * ML_QK_DIM)
        vv = slice(h * ML_V_DIM, (h + 1) * ML_V_DIM)
        q = q_ref[:, qk]
        k = k_ref[:, qk]
        v = v_ref[:, vv]
        c_prev = c_sc[h]
        n_prev = n_sc[h:h + 1, :]
        m_prev = m_sc[h:h + 1, 0:1]

        b_col = b_cols[:, ML_HEADS + h:ML_HEADS + h + 1]
        i_col = g[:, h:h + 1]
        b_row = b_rows[ML_HEADS + h:ML_HEADS + h + 1, :]
        i_row = g_rows[h:h + 1, :]

        d_log = jnp.where(causal, b_col - b_row + i_row, -jnp.inf)
        inter_log = b_col + m_prev
        m_row = jnp.maximum(inter_log, jnp.max(d_log, axis=-1, keepdims=True))
        w_intra = jnp.exp(d_log - m_row)
        w_inter = jnp.exp(inter_log - m_row)

        s = lax.dot_general(q, k, (((1,), (1,)), ((), ())), preferred_element_type=F32) * w_intra
        inter = lax.dot_general(q, c_prev.astype(BF16), (((1,), (1,)), ((), ())), preferred_element_type=F32)
        num = w_inter * inter + jnp.dot(s.astype(BF16), v, preferred_element_type=F32)
        den = w_inter * jnp.sum(q.astype(F32) * n_prev, axis=-1, keepdims=True) + jnp.sum(s, axis=-1, keepdims=True)
        hid = num / jnp.maximum(jnp.abs(den), jnp.exp(-m_row))

        m_new = m_row[chunk - 1:chunk, :]
        b_last = b_col[chunk - 1:chunk, :]
        w_prev = jnp.exp(b_last + m_prev - m_new)
        w_tok = jnp.exp(b_last - b_col + i_col - m_new)
        vw = (v.astype(F32) * w_tok).astype(BF16)
        c_sc[h] = w_prev * c_prev + lax.dot_general(vw, k, (((0,), (0,)), ((), ())), preferred_element_type=F32)
        n_sc[h:h + 1, :] = w_prev * n_prev + jnp.sum(k.astype(F32) * w_tok, axis=0, keepdims=True)
        m_sc[h:h + 1, :] = jnp.broadcast_to(m_new, (1, m_sc.shape[1]))

        normed = _head_rmsnorm(hid, gain_ref[h:h + 1, :])
        out_ref[:, vv] = (normed * _sigmoid(o_ref[:, vv])).astype(out_ref.dtype)

    @pl.when(ci == nci - 1)
    def _():
        c_out_ref[...] = c_sc[...]
        n_out_ref[...] = n_sc[...]
        m_out_ref[...] = m_sc[...]


def _mlstm(q, k, v, o, g, gain, c0, n0, m0, mixed, *, batch, seq, chunk, row_offset, col_offset):
    nci = seq // chunk
    roff = row_offset // chunk
    vw = ML_HEADS * ML_V_DIM
    qw = ML_HEADS * ML_QK_DIM
    coff = col_offset // vw
    state_bytes = (_nbytes((ML_HEADS, ML_V_DIM, ML_QK_DIM), F32) + _nbytes((ML_HEADS, ML_QK_DIM), F32)
                   + _nbytes((ML_HEADS, V7X_LANES), F32))
    blocks = (2 * _nbytes((chunk, qw), BF16) + 2 * _nbytes((chunk, vw), BF16) + _nbytes((chunk, vw), F32)
              + _nbytes((chunk, V7X_LANES), F32) + 2 * state_bytes)
    row_map = lambda b, c: (b * nci + c + roff, 0)
    state_specs = [pl.BlockSpec((None, ML_HEADS, ML_V_DIM, ML_QK_DIM), lambda b, c: (b, 0, 0, 0)),
                   pl.BlockSpec((None, ML_HEADS, ML_QK_DIM), lambda b, c: (b, 0, 0)),
                   pl.BlockSpec((None, ML_HEADS, V7X_LANES), lambda b, c: (b, 0, 0))]
    return pl.pallas_call(
        functools.partial(_mlstm_kernel, chunk=chunk),
        out_shape=[jax.ShapeDtypeStruct(mixed.shape, mixed.dtype),
                   jax.ShapeDtypeStruct(c0.shape, F32),
                   jax.ShapeDtypeStruct(n0.shape, F32),
                   jax.ShapeDtypeStruct(m0.shape, F32)],
        grid=(batch, nci),
        in_specs=[pl.BlockSpec((chunk, qw), row_map),
                  pl.BlockSpec((chunk, qw), row_map),
                  pl.BlockSpec((chunk, vw), row_map),
                  pl.BlockSpec((chunk, vw), row_map),
                  pl.BlockSpec((chunk, V7X_LANES), row_map),
                  pl.BlockSpec((ML_HEADS, ML_V_DIM), lambda b, c: (0, 0)),
                  *state_specs,
                  pl.BlockSpec(memory_space=pl.ANY)],
        out_specs=[pl.BlockSpec((chunk, vw), lambda b, c: (b * nci + c + roff, coff)), *state_specs],
        scratch_shapes=[pltpu.VMEM((ML_HEADS, ML_V_DIM, ML_QK_DIM), F32),
                        pltpu.VMEM((ML_HEADS, ML_QK_DIM), F32),
                        pltpu.VMEM((ML_HEADS, V7X_LANES), F32)],
        input_output_aliases={9: 0},
        compiler_params=pltpu.CompilerParams(
            dimension_semantics=("parallel", "arbitrary"),
            vmem_limit_bytes=_vmem_limit(blocks, scratch_bytes=state_bytes, temp_bytes=12 << 20)),
        name="mlstm",
    )(q, k, v, o, g, gain, c0, n0, m0, mixed)


def _pick_row_tile(t, target):
    best = 16
    for cand in range(16, target + 1, 16):
        if t % cand == 0:
            best = cand
    return best


def kernel(x_prompt, x_sample, cache_sb_k, cache_sb_v, state_ml_c, state_ml_n, state_ml_m, p_prompt, p_sample, ln_ffn1, w_ffn1_gate, w_ffn1_up, w_ffn1_down, ln_mix, w_in, b_if, g_sb_head, g_ml_head, w_out, ln_ffn2, w_ffn2_gate, w_ffn2_up, w_ffn2_down, ln_ple, w_ple_gate, w_ple_proj, ln_final):
    depth = w_in.shape[0]
    batch, seq, d_model = x_prompt.shape
    dec_batch, dec_seq, _ = x_sample.shape
    tp = batch * seq
    ts = dec_batch * dec_seq
    t_all = tp + ts
    sb_heads = g_sb_head.shape[1]
    sb_width = sb_heads * SB_HEAD_DIM
    ml_qk_width = ML_HEADS * ML_QK_DIM
    ml_v_width = ML_HEADS * ML_V_DIM
    mix_width = sb_width + ml_v_width
    off_q, off_k, off_v = 0, sb_width, 2 * sb_width
    off_mq = 3 * sb_width
    off_mk = off_mq + ml_qk_width
    off_mv = off_mk + ml_qk_width
    off_mo = off_mv + ml_v_width
    off_g = off_mo + ml_v_width

    tm_tall = _pick_row_tile(t_all, MATMUL_ROW_TILE_TARGET)
    tm_prompt = _pick_row_tile(tp, MATMUL_ROW_TILE_TARGET)
    norm_tile = _pick_row_tile(t_all, NORM_ROW_TILE_TARGET)
    mm_tn = V7X_MXU_DIM
    sb_tile = V7X_MXU_DIM
    ml_chunk = min(seq, V7X_MXU_DIM)
    tall = dict(tm=tm_tall, single_buffer_lhs=True)

    x, h = _stack_tokens(x_prompt.reshape(tp, d_model), x_sample.reshape(ts, d_model), ln_ffn1[0],
                         tile=math.gcd(tp, ts))

    def half_ffn(x, h, w_gate, w_up, w_down):
        a = _ffn_gate_up(h, w_gate, w_up, tm=tm_tall, tn=mm_tn, tk=DOWN_K_TILE)
        return _ffn_down(a, w_down, x, tm=tm_tall, tn=DOWN_COL_TILE)

    outs = {name: [] for name in ("pk", "pv", "pc", "pn", "pm", "sk", "sv", "sc", "sn", "sm")}
    for l in range(depth):
        if l > 0:
            h = _rmsnorm(x, ln_ffn1[l], BF16, rows=t_all, row_tile=norm_tile)
        x = half_ffn(x, h, w_ffn1_gate[l], w_ffn1_up[l], w_ffn1_down[l])

        h = _rmsnorm(x, ln_mix[l], BF16, rows=t_all, row_tile=norm_tile)
        w = jnp.swapaxes(w_in[l], 0, 1)
        proj = functools.partial(_matmul_fullk, h, w_transposed=True)
        (sb_q,) = proj([w], [off_q], sb_width, [BF16], _ep_sb_q, tm=tm_tall, tn=mm_tn, name="proj_sb_q")
        k_p, v_p, k_pb, v_pb = proj([w, w], [off_k, off_v], sb_width, [F32, F32, BF16, BF16], _ep_kv,
                                    tm=tm_prompt, single_buffer_lhs=True, tn=mm_tn, rows=tp, name="proj_sb_kv_prompt")
        k_s, v_s = proj([w, w], [off_k, off_v], sb_width, [F32, F32], _ep_identity,
                        tm=ts, tn=mm_tn, rows=ts, row_offset=tp, name="proj_sb_kv_sample")
        ml_q, ml_k = proj([w, w], [off_mq, off_mk], ml_qk_width, [BF16, BF16], _ep_ml_qk,
                          tn=mm_tn, name="proj_ml_qk", **tall)
        ml_v, ml_o = proj([w, w], [off_mv, off_mo], ml_v_width, [BF16, F32], _ep_identity,
                          tn=mm_tn, name="proj_ml_vo", **tall)
        b_row = jnp.pad(b_if[l], (0, V7X_LANES - N_GATES)).reshape(1, V7X_LANES)
        (gates,) = proj([w], [off_g], V7X_LANES, [F32], _ep_gates, extras=(b_row,),
                        tm=tm_tall, tn=V7X_LANES, name="proj_ml_gates")

        assert h.shape == (t_all, mix_width) and h.dtype == BF16
        mixed = _sb_prompt(sb_q, k_pb, v_pb, g_sb_head[l], h, batch=batch, seq=seq,
                           tile=sb_tile, heads_per_step=SB_HEADS_PER_STEP)
        mixed = _sb_sample(sb_q, k_s, v_s, cache_sb_k, cache_sb_v, g_sb_head[l], mixed,
                           layer=l, row_offset=tp, tile=sb_tile)
        c0 = jnp.zeros((batch, ML_HEADS, ML_V_DIM, ML_QK_DIM), F32)
        n0 = jnp.zeros((batch, ML_HEADS, ML_QK_DIM), F32)
        m0 = jnp.zeros((batch, ML_HEADS, V7X_LANES), F32)
        mixed, c_p, n_p, m_p = _mlstm(ml_q, ml_k, ml_v, ml_o, gates, g_ml_head[l], c0, n0, m0, mixed,
                                      batch=batch, seq=seq, chunk=ml_chunk, row_offset=0, col_offset=sb_width)
        m0_s = jnp.broadcast_to(state_ml_m[l][:, :, None], (dec_batch, ML_HEADS, V7X_LANES))
        mixed, c_s, n_s, m_s = _mlstm(ml_q, ml_k, ml_v, ml_o, gates, g_ml_head[l], state_ml_c[l], state_ml_n[l],
                                      m0_s, mixed, batch=dec_batch, seq=dec_seq, chunk=dec_seq,
                                      row_offset=tp, col_offset=sb_width)

        (x,) = _matmul_fullk(mixed, [w_out[l]], [0], d_model, [F32], _ep_residual, extras=(x,),
                             tn=mm_tn, name="out_proj", **tall)
        h = _rmsnorm(x, ln_ffn2[l], BF16, rows=t_all, row_tile=norm_tile)
        x = half_ffn(x, h, w_ffn2_gate[l], w_ffn2_up[l], w_ffn2_down[l])

        h = _rmsnorm(x, ln_ple[l], BF16, rows=t_all, row_tile=norm_tile)
        p = jnp.concatenate([p_prompt[l].reshape(tp, -1), p_sample[l].reshape(ts, -1)], axis=0)
        x = _ple(h, w_ple_gate[l], p, w_ple_proj[l], x, tm=tm_tall, tn=mm_tn)

        outs["pk"].append(k_p.reshape(batch, seq, sb_heads, SB_HEAD_DIM))
        outs["pv"].append(v_p.reshape(batch, seq, sb_heads, SB_HEAD_DIM))
        outs["pc"].append(c_p)
        outs["pn"].append(n_p)
        outs["pm"].append(m_p[:, :, 0])
        outs["sk"].append(k_s.reshape(dec_batch, dec_seq, sb_heads, SB_HEAD_DIM))
        outs["sv"].append(v_s.reshape(dec_batch, dec_seq, sb_heads, SB_HEAD_DIM))
        outs["sc"].append(c_s)
        outs["sn"].append(n_s)
        outs["sm"].append(m_s[:, :, 0])

    y_prompt = _rmsnorm(x, ln_final, F32, rows=tp, row_tile=FINAL_NORM_ROW_TILE).reshape(batch, seq, d_model)
    y_sample = _rmsnorm(x, ln_final, F32, rows=ts, row_tile=ts, row_offset=tp).reshape(dec_batch, dec_seq, d_model)

    def stack(name):
        vals = outs[name]
        return vals[0][None] if len(vals) == 1 else jnp.stack(vals)

    return (y_prompt, y_sample, stack("pk"), stack("pv"), stack("pc"), stack("pn"), stack("pm"),
            stack("sk"), stack("sv"), stack("sc"), stack("sn"), stack("sm"))
```
